```python
import math
import jax
import jax.numpy as jnp
from jax import lax
import numpy as np

D_MODEL = 1024
BATCH = 8
SEQ = 2048
DEPTH = 2
DEC_BATCH = 32
DEC_SEQ = 64
PAST_LEN = 2048

CHUNK = 64
N_META = 16
MIX_WIDTH = 1024
POOL_WINDOWS = (2, 4, 8, 16)
POOL_GROUPS = len(POOL_WINDOWS)
POOL_GROUP_WIDTH = MIX_WIDTH // POOL_GROUPS
POOL_MAX = max(POOL_WINDOWS)
SB_HEADS = 8
SB_HEAD_DIM = MIX_WIDTH // SB_HEADS
SB_SCALE = SB_HEAD_DIM ** -0.5
QBLK = 128
HG_HEADS = 8
HG_DK = 128
HG_DV = MIX_WIDTH // HG_HEADS
N_BRANCH = 3
N_SPLIT = 13
EPS = 1e-6
LB_FLOOR = 1e-30

kernel_name = 'hybrid_pool_stickbreak_hgrn2_stream'


def rms_norm(x, g):
    xf = x.astype(jnp.float32)
    y = xf * lax.rsqrt(jnp.mean(xf * xf, axis=-1, keepdims=True) + EPS)
    return (y * g.astype(jnp.float32)).astype(x.dtype)


def pool_mixer(u, prefix, pos0, w_group, scale):
    b, l, _ = u.shape
    p = POOL_MAX - 1
    full = jnp.concatenate([prefix.astype(u.dtype), u], axis=1)
    csum = jnp.pad(jnp.cumsum(full.astype(jnp.float32), axis=1), ((0, 0), (1, 0), (0, 0)))
    pos = (pos0 + jnp.arange(l)).astype(jnp.float32)
    hi = csum[:, p + 1:p + 1 + l]
    means = []
    for gi, w in enumerate(POOL_WINDOWS):
        sl = slice(gi * POOL_GROUP_WIDTH, (gi + 1) * POOL_GROUP_WIDTH)
        lo = csum[:, p + 1 - w:p + 1 - w + l, sl]
        cnt = jnp.minimum(pos + 1.0, float(w))
        means.append((hi[..., sl] - lo) / cnt[None, :, None])
    pooled = jnp.stack(means, axis=2)
    diff = pooled - u.astype(jnp.float32).reshape(b, l, POOL_GROUPS, POOL_GROUP_WIDTH)
    y = jnp.einsum('blgc,gce->blge', diff.astype(u.dtype), w_group).reshape(b, l, MIX_WIDTH)
    return y * scale, full[:, -p:]


def stick_breaking(q, k, v, q_pos0):
    b, lq, h, d = q.shape
    lk = k.shape[1]
    qb = min(QBLK, lq)
    nblk = -(-lq // qb)
    qp = jnp.pad(q, ((0, 0), (0, nblk * qb - lq), (0, 0), (0, 0)))
    qblocks = qp.reshape(b, nblk, qb, h, d).transpose(1, 0, 2, 3, 4)
    kpos = jnp.arange(lk)

    def block(args):
        qblk, start = args
        z = jnp.einsum('bqhd,bkhd->bhqk', qblk, k).astype(jnp.float32) * SB_SCALE
        qpos = q_pos0 + start + jnp.arange(qb)
        mask = kpos[None, :] < qpos[:, None]
        log_keep = jnp.where(mask, jax.nn.log_sigmoid(-z), 0.0)
        later = lax.cumsum(log_keep, axis=3, reverse=True) - log_keep
        wts = jnp.where(mask, jnp.exp(jax.nn.log_sigmoid(z) + later), 0.0)
        return jnp.einsum('bhqk,bkhd->bqhd', wts.astype(v.dtype), v)

    out = lax.map(block, (qblocks, jnp.arange(nblk) * qb))
    return out.transpose(1, 0, 2, 3, 4).reshape(b, nblk * qb, h, d)[:, :lq]


def hgrn2(q, log_f, k, v, state0):
    b, l, h, _ = q.shape
    c = min(CHUNK, l)
    n = -(-l // c)
    pad = n * c - l
    def blocks(a):
        a = jnp.pad(a.astype(jnp.float32), ((0, 0), (0, pad), (0, 0), (0, 0)))
        return a.reshape(b, n, c, h, a.shape[-1]).transpose(1, 0, 3, 2, 4)
    causal = jnp.tril(jnp.ones((c, c), dtype=bool))[:, :, None]

    def step(s_prev, inp):
        qc, gc, kc, vc = inp
        cum = jnp.cumsum(gc, axis=2)
        o_inter = jnp.einsum('bhtk,bhkv->bhtv', qc * jnp.exp(cum), s_prev)
        gap = cum[:, :, :, None, :] - cum[:, :, None, :, :]
        decay = jnp.where(causal, jnp.exp(jnp.where(causal, gap, 0.0)), 0.0)
        att = jnp.einsum('bhtk,bhsk,bhtsk->bhts', qc, kc, decay)
        o = o_inter + jnp.einsum('bhts,bhsv->bhtv', att, vc)
        last = cum[:, :, -1:, :]
        s_new = jnp.exp(last[:, :, 0, :])[..., None] * s_prev + jnp.einsum('bhsk,bhsv->bhkv', kc * jnp.exp(last - cum), vc)
        return s_new, o

    s_fin, o = lax.scan(step, state0.astype(jnp.float32), (blocks(q), blocks(log_f), blocks(k), blocks(v)))
    o = o.transpose(1, 0, 3, 2, 4).reshape(b, n * c, h, -1)[:, :l]
    return o, s_fin


def mixer_layer(h, pos0, pool_prefix, k_past, v_past, hg_state, lb,
                norm_g, w_in, q_norm_g, k_norm_g, w_pool, pool_scale, hg_norm_g, w_branch, w_out):
    b, l, _ = h.shape
    xn = rms_norm(h, norm_g)
    (u_a, g_a, q_b, k_b, v_b, g_b, f_c, q_c, i_c, g_c,
     m_a, m_b, m_c) = jnp.split(xn @ w_in, N_SPLIT, axis=-1)
    heads = lambda t, d: t.reshape(b, l, -1, d)

    y_a, new_prefix = pool_mixer(u_a, pool_prefix, pos0, w_pool, pool_scale)
    y_a = y_a * jax.nn.silu(g_a)

    q = rms_norm(heads(q_b, SB_HEAD_DIM), q_norm_g)
    k = rms_norm(heads(k_b, SB_HEAD_DIM), k_norm_g)
    v = heads(v_b, SB_HEAD_DIM)
    if k_past is None:
        k_all, v_all, q_pos0 = k, v, 0
    else:
        k_all = jnp.concatenate([k_past.astype(k.dtype), k], axis=1)
        v_all = jnp.concatenate([v_past.astype(v.dtype), v], axis=1)
        q_pos0 = k_past.shape[1]
    y_b = stick_breaking(q, k_all, v_all, q_pos0).reshape(b, l, MIX_WIDTH) * jax.nn.silu(g_b)

    z = heads(f_c, HG_DK).astype(jnp.float32)
    lbh = lb.reshape(HG_HEADS, HG_DK).astype(jnp.float32)
    log_f = jnp.logaddexp(jnp.log(jnp.maximum(lbh, LB_FLOOR)), jnp.log1p(-lbh) + jax.nn.log_sigmoid(z))
    k_c = (1.0 - lbh) * jax.nn.sigmoid(-z)
    o_c, new_state = hgrn2(jax.nn.silu(heads(q_c, HG_DK)), log_f, k_c, heads(i_c, HG_DV), hg_state)
    o_c = rms_norm(o_c, hg_norm_g).astype(h.dtype).reshape(b, l, MIX_WIDTH)
    y_c = o_c * jax.nn.silu(g_c)

    ys = jnp.stack([y_a, y_b, y_c], axis=2)
    gates = jax.nn.sigmoid(jnp.stack([m_a, m_b, m_c], axis=2))
    merged = jnp.sum(gates * jnp.einsum('blnw,nwd->blnd', ys, w_branch), axis=2)
    h = h + merged @ w_out
    return h, k, v, new_prefix, new_state


def setup_inputs(seed: int = 0) -> dict:
    key = jax.random.key(seed)
    ks = jax.random.split(key, 17)
    nrm = lambda kk, shape, s: jax.random.normal(kk, shape, jnp.float32) * s
    return {
        'x_prompt': nrm(ks[0], (BATCH, SEQ, D_MODEL), 1.0),
        'x_sample': nrm(ks[1], (DEC_BATCH, DEC_SEQ, D_MODEL), 1.0),
        'cache_k': nrm(ks[2], (DEPTH, DEC_BATCH, PAST_LEN, SB_HEADS, SB_HEAD_DIM), 1.0),
        'cache_v': nrm(ks[3], (DEPTH, DEC_BATCH, PAST_LEN, SB_HEADS, SB_HEAD_DIM), 0.5),
        'state_pool': nrm(ks[4], (DEPTH, DEC_BATCH, POOL_MAX - 1, MIX_WIDTH), 1.0),
        'state_hgrn': nrm(ks[5], (DEPTH, DEC_BATCH, HG_HEADS, HG_DK, HG_DV), 0.5),
        'meta_tokens': nrm(ks[6], (N_META, D_MODEL), 1.0),
        'norm_g': 1.0 + nrm(ks[7], (DEPTH, D_MODEL), 0.05),
        'w_in': nrm(ks[8], (DEPTH, D_MODEL, N_SPLIT * MIX_WIDTH), D_MODEL ** -0.5),
        'q_norm_g': 1.0 + nrm(ks[9], (DEPTH, SB_HEAD_DIM), 0.05),
        'k_norm_g': 1.0 + nrm(ks[10], (DEPTH, SB_HEAD_DIM), 0.05),
        'w_pool': nrm(ks[11], (DEPTH, POOL_GROUPS, POOL_GROUP_WIDTH, POOL_GROUP_WIDTH), POOL_GROUP_WIDTH ** -0.5),
        'pool_scale': 1.0 + nrm(ks[12], (DEPTH, MIX_WIDTH), 0.05),
        'hgrn_lower_bounds': nrm(ks[13], (DEPTH, HG_HEADS * HG_DK), 0.1),
        'hgrn_norm_g': 1.0 + nrm(ks[14], (DEPTH, HG_DV), 0.05),
        'w_branch': nrm(ks[15], (DEPTH, N_BRANCH, MIX_WIDTH, D_MODEL), MIX_WIDTH ** -0.5),
        'w_out': nrm(ks[16], (DEPTH, D_MODEL, D_MODEL), D_MODEL ** -0.5),
    }


def reference(x_prompt, x_sample, cache_k, cache_v, state_pool, state_hgrn, meta_tokens,
              norm_g, w_in, q_norm_g, k_norm_g, w_pool, pool_scale, hgrn_lower_bounds,
              hgrn_norm_g, w_branch, w_out):
    lb_soft = jax.nn.softmax(hgrn_lower_bounds.astype(jnp.float32), axis=0)
    lower_bounds = jnp.cumsum(lb_soft, axis=0) - lb_soft[0:1]

    bp = x_prompt.shape[0]
    meta = jnp.broadcast_to(meta_tokens.astype(x_prompt.dtype)[None], (bp, N_META, D_MODEL))
    hp = jnp.concatenate([meta, x_prompt], axis=1)
    hs = x_sample
    past = cache_k.shape[2]
    kp, vp, pp, sp, ksm, vsm, psm, ssm = [], [], [], [], [], [], [], []
    for l in range(DEPTH):
        wl = (norm_g[l], w_in[l], q_norm_g[l], k_norm_g[l], w_pool[l], pool_scale[l],
              hgrn_norm_g[l], w_branch[l], w_out[l])
        hp, k_, v_, p_, s_ = mixer_layer(
            hp, 0, jnp.zeros((bp, POOL_MAX - 1, MIX_WIDTH), hp.dtype), None, None,
            jnp.zeros((bp, HG_HEADS, HG_DK, HG_DV), jnp.float32), lower_bounds[l], *wl)
        kp.append(k_); vp.append(v_); pp.append(p_); sp.append(s_.astype(hp.dtype))
        hs, k_, v_, p_, s_ = mixer_layer(
            hs, past, state_pool[l], cache_k[l], cache_v[l], state_hgrn[l], lower_bounds[l], *wl)
        ksm.append(k_); vsm.append(v_); psm.append(p_); ssm.append(s_.astype(state_hgrn.dtype))
    y_prompt = hp[:, N_META:]
    return (y_prompt, hs, jnp.stack(kp), jnp.stack(vp), jnp.stack(pp), jnp.stack(sp),
            jnp.stack(ksm), jnp.stack(vsm), jnp.stack(psm), jnp.stack(ssm))
```

```python
import functools

import jax
import jax.numpy as jnp
from jax import lax
from jax.experimental import pallas as pl
from jax.experimental.pallas import tpu as pltpu

F32 = jnp.float32
BF16 = jnp.bfloat16

D_MODEL = 1024
WIDTH = 1024
N_SPLIT = 13
N_META = 16
POOL_WINDOWS = (2, 4, 8, 16)
POOL_GW = WIDTH // len(POOL_WINDOWS)
POOL_MAX = max(POOL_WINDOWS)
HEADS = 8
HEAD_DIM = WIDTH // HEADS
SB_SCALE = HEAD_DIM ** -0.5
EPS = 1e-6
LB_FLOOR = 1e-30
HG_CHUNK = 64
HG_SUB = 16
LANE = 128
VMEM_LIMIT = 56 * 1024 * 1024

C_UA, C_GA, C_QB, C_KB, C_VB, C_GB, C_FC, C_QC, C_IC, C_GC, C_MA, C_MB, C_MC = range(N_SPLIT)


def _sigmoid(x):
    return 1.0 / (1.0 + jnp.exp(-x))


def _softplus(x):
    return jnp.maximum(x, 0.0) + jnp.log1p(jnp.exp(-jnp.abs(x)))


def _params(*sem):
    return pltpu.CompilerParams(dimension_semantics=sem, vmem_limit_bytes=VMEM_LIMIT)


def _in_proj_kernel(x_ref, g_ref, w_ref, qg_ref, kg_ref, o_ref, xn_ref):
    j = pl.program_id(1)

    @pl.when(j == 0)
    def _():
        x = x_ref[...]
        ms = jnp.mean(x * x, axis=-1, keepdims=True)
        xn_ref[...] = (x * lax.rsqrt(ms + EPS) * g_ref[...]).astype(BF16)

    acc = jnp.dot(xn_ref[...], w_ref[...], preferred_element_type=F32)
    is_q = j == C_QB
    is_qk = jnp.logical_or(is_q, j == C_KB)

    @pl.when(is_qk)
    def _():
        gain = jnp.where(is_q, qg_ref[...], kg_ref[...])
        for h in range(HEADS):
            hs = slice(h * HEAD_DIM, (h + 1) * HEAD_DIM)
            a = acc[:, hs]
            ms = jnp.mean(a * a, axis=-1, keepdims=True)
            o_ref[:, hs] = a * lax.rsqrt(ms + EPS) * gain

    @pl.when(jnp.logical_not(is_qk))
    def _():
        o_ref[...] = acc


def _pick_tile(t, candidates):
    return next(c for c in candidates if t % c == 0)


def _in_proj(h_all, norm_g, w_bf, q_g, k_g):
    t = h_all.shape[0]
    tm = _pick_tile(t, (1024, 512, 256, 128))
    return pl.pallas_call(
        _in_proj_kernel,
        grid=(t // tm, N_SPLIT),
        in_specs=[
            pl.BlockSpec((tm, D_MODEL), lambda i, j: (i, 0)),
            pl.BlockSpec((1, D_MODEL), lambda i, j: (0, 0)),
            pl.BlockSpec((D_MODEL, WIDTH), lambda i, j: (0, j)),
            pl.BlockSpec((1, HEAD_DIM), lambda i, j: (0, 0)),
            pl.BlockSpec((1, HEAD_DIM), lambda i, j: (0, 0)),
        ],
        out_specs=pl.BlockSpec((tm, WIDTH), lambda i, j: (i, j)),
        out_shape=jax.ShapeDtypeStruct((t, N_SPLIT * WIDTH), F32),
        scratch_shapes=[pltpu.VMEM((tm, D_MODEL), BF16)],
        compiler_params=_params("parallel", "arbitrary"),
        name="in_proj",
    )(h_all, norm_g.reshape(1, D_MODEL), w_bf, q_g.reshape(1, HEAD_DIM), k_g.reshape(1, HEAD_DIM))


def _pool_kernel(*refs, seq, pos0, has_prefix):
    if has_prefix:
        u_ref, p_ref, w_ref, s_ref, o_ref = refs
    else:
        u_ref, w_ref, s_ref, o_ref = refs
    g = pl.program_id(1)
    u = u_ref[...]
    head = p_ref[0] if has_prefix else jnp.zeros((POOL_MAX, POOL_GW), F32)
    x = jnp.concatenate([head, u], axis=0)
    pos = pos0 + lax.broadcasted_iota(jnp.int32, (seq, 1), 0)

    for gi, win in enumerate(POOL_WINDOWS):
        @pl.when(g == gi)
        def _(win=win):
            s = x
            sh = 1
            while sh < win:
                s = s + pltpu.roll(s, sh, 0)
                sh *= 2
            cnt = jnp.minimum(pos + 1, win).astype(F32)
            diff = s[POOL_MAX:] / cnt - u
            y = jnp.dot(diff.astype(BF16), w_ref[0], preferred_element_type=F32)
            o_ref[...] = y * s_ref[...]


def _pool(proj, prefix16, w_pool_bf, scale, *, nb, seq, row0, pos0):
    t = proj.shape[0]
    rb0 = row0 // seq
    has_prefix = prefix16 is not None
    in_specs = [pl.BlockSpec((seq, POOL_GW), lambda b, g: (rb0 + b, g))]
    args = [proj]
    if has_prefix:
        in_specs.append(pl.BlockSpec((1, POOL_MAX, POOL_GW), lambda b, g: (b, 0, g)))
        args.append(prefix16)
    in_specs += [
        pl.BlockSpec((1, POOL_GW, POOL_GW), lambda b, g: (g, 0, 0)),
        pl.BlockSpec((1, POOL_GW), lambda b, g: (0, g)),
    ]
    args += [w_pool_bf, scale.reshape(1, WIDTH)]
    return pl.pallas_call(
        functools.partial(_pool_kernel, seq=seq, pos0=pos0, has_prefix=has_prefix),
        grid=(nb, len(POOL_WINDOWS)),
        in_specs=in_specs,
        out_specs=pl.BlockSpec((seq, POOL_GW), lambda b, g: (b, g)),
        out_shape=jax.ShapeDtypeStruct((nb * seq, WIDTH), F32),
        compiler_params=_params("parallel", "parallel"),
        name="pool",
    )(*args)


def _tri_after(n):
    r = lax.broadcasted_iota(jnp.int32, (n, n), 0)
    c = lax.broadcasted_iota(jnp.int32, (n, n), 1)
    return jnp.where(r > c, 1.0, 0.0).astype(BF16)


def _sb_tile(q, k, v, carry, tri, causal):
    z = lax.dot_general(q, k, (((1,), (1,)), ((), ())), preferred_element_type=F32) * SB_SCALE
    sp = _softplus(z)
    log_keep = -sp
    if causal:
        r = lax.broadcasted_iota(jnp.int32, z.shape, 0)
        c = lax.broadcasted_iota(jnp.int32, z.shape, 1)
        mask = c < r
        log_keep = jnp.where(mask, log_keep, 0.0)
    hi = log_keep.astype(BF16)
    lo = (log_keep - hi.astype(F32)).astype(BF16)
    later = (jnp.dot(hi, tri, preferred_element_type=F32)
             + jnp.dot(lo, tri, preferred_element_type=F32))
    if carry is not None:
        later = later + carry
    wts = jnp.exp(z - sp + later)
    if causal:
        wts = jnp.where(mask, wts, 0.0)
    pv = jnp.dot(wts.astype(BF16), v, preferred_element_type=F32)
    return pv, jnp.sum(log_keep, axis=1, keepdims=True)


def _attn_prompt_kernel(q_ref, k_ref, v_ref, o_ref):
    i = pl.program_id(1)
    tq = q_ref.shape[0]
    tri = _tri_after(tq)
    for h in range(HEADS):
        hs = slice(h * HEAD_DIM, (h + 1) * HEAD_DIM)
        q = q_ref[:, hs].astype(BF16)
        d0 = pl.multiple_of(i * tq, tq)
        acc, carry = _sb_tile(q, k_ref[pl.ds(d0, tq), hs], v_ref[pl.ds(d0, tq), hs], None, tri, True)

        def body(t, state, q=q, hs=hs):
            acc, carry = state
            k0 = pl.multiple_of((i - 1 - t) * tq, tq)
            pv, rs = _sb_tile(q, k_ref[pl.ds(k0, tq), hs], v_ref[pl.ds(k0, tq), hs], carry, tri, False)
            return acc + pv, carry + rs

        acc, _ = lax.fori_loop(0, i, body, (acc, carry))
        o_ref[:, hs] = acc


def _attn_prompt(proj, k_bf, v_bf, *, nb, seq):
    nq = seq // LANE
    return pl.pallas_call(
        _attn_prompt_kernel,
        grid=(nb, nq),
        in_specs=[
            pl.BlockSpec((LANE, WIDTH), lambda b, i: (b * nq + i, C_QB)),
            pl.BlockSpec((seq, WIDTH), lambda b, i: (b, 0)),
            pl.BlockSpec((seq, WIDTH), lambda b, i: (b, 0)),
        ],
        out_specs=pl.BlockSpec((LANE, WIDTH), lambda b, i: (b * nq + i, 0)),
        out_shape=jax.ShapeDtypeStruct((nb * seq, WIDTH), F32),
        compiler_params=_params("parallel", "arbitrary"),
        name="attn_prompt",
    )(proj, k_bf, v_bf)


def _attn_sample_kernel(q_ref, kn_ref, vn_ref, kp_ref, vp_ref, o_ref, *, past):
    tq = q_ref.shape[0]
    tri_new = _tri_after(tq)
    tri = _tri_after(LANE)
    n_past = past // LANE
    for h in range(HEADS):
        hs = slice(h * HEAD_DIM, (h + 1) * HEAD_DIM)
        q = q_ref[:, hs].astype(BF16)
        acc, carry = _sb_tile(q, kn_ref[:, hs].astype(BF16), vn_ref[:, hs].astype(BF16), None, tri_new, True)

        def body(t, state, q=q, hs=hs):
            acc, carry = state
            k0 = pl.multiple_of((n_past - 1 - t) * LANE, LANE)
            k = kp_ref[0, 0, pl.ds(k0, LANE), hs].astype(BF16)
            v = vp_ref[0, 0, pl.ds(k0, LANE), hs].astype(BF16)
            pv, rs = _sb_tile(q, k, v, carry, tri, False)
            return acc + pv, carry + rs

        acc, _ = lax.fori_loop(0, n_past, body, (acc, carry))
        o_ref[:, hs] = acc


def _attn_sample(proj, cache_k, cache_v, layer, *, nb, seq, row0):
    past = cache_k.shape[2]
    rb0 = row0 // seq
    return pl.pallas_call(
        functools.partial(_attn_sample_kernel, past=past),
        grid=(nb,),
        in_specs=[
            pl.BlockSpec((seq, WIDTH), lambda b: (rb0 + b, C_QB)),
            pl.BlockSpec((seq, WIDTH), lambda b: (rb0 + b, C_KB)),
            pl.BlockSpec((seq, WIDTH), lambda b: (rb0 + b, C_VB)),
            pl.BlockSpec((1, 1, past, WIDTH), lambda b: (layer, b, 0, 0)),
            pl.BlockSpec((1, 1, past, WIDTH), lambda b: (layer, b, 0, 0)),
        ],
        out_specs=pl.BlockSpec((seq, WIDTH), lambda b: (b, 0)),
        out_shape=jax.ShapeDtypeStruct((nb * seq, WIDTH), F32),
        compiler_params=_params("parallel"),
        name="attn_sample",
    )(proj, proj, proj, cache_k, cache_v)


def _cumsum_rows(x):
    n = x.shape[0]
    row = lax.broadcasted_iota(jnp.int32, (n, 1), 0)
    sh = 1
    while sh < n:
        x = x + jnp.where(row >= sh, pltpu.roll(x, sh, 0), 0.0)
        sh *= 2
    return x


def _hgrn_kernel(*refs, layer, valid_len, has_state):
    if has_state:
        z_ref, q_ref, v_ref, lbp_ref, ng_ref, s0_ref, o_ref, sf_ref, st_ref = refs
    else:
        z_ref, q_ref, v_ref, lbp_ref, ng_ref, o_ref, sf_ref, st_ref = refs
    c = pl.program_id(1)
    nc = pl.num_programs(1)
    n = HG_CHUNK

    lbp = lbp_ref[...]
    e = jnp.exp(lbp - jnp.max(lbp, axis=0, keepdims=True))
    soft = e / jnp.sum(e, axis=0, keepdims=True)
    run = soft[0:1]
    for l in range(1, layer + 1):
        run = run + soft[l:l + 1]
    lb_all = run - soft[0:1]

    row = lax.broadcasted_iota(jnp.int32, (n, 1), 0)
    valid = None if valid_len is None else (c * n + row) < valid_len
    sub = row % HG_SUB

    @pl.when(c == 0)
    def _():
        for h in range(HEADS):
            st_ref[h] = s0_ref[0, h].T if has_state else jnp.zeros((HEAD_DIM, HEAD_DIM), F32)

    for h in range(HEADS):
        hs = slice(h * HEAD_DIM, (h + 1) * HEAD_DIM)
        z = z_ref[:, hs]
        qc = q_ref[:, hs]
        vv = v_ref[:, hs]
        lb = lb_all[:, hs]
        a = jnp.log(jnp.maximum(lb, LB_FLOOR))
        b = jnp.log1p(-lb) + (jnp.minimum(z, 0.0) - jnp.log1p(jnp.exp(-jnp.abs(z))))
        log_f = jnp.maximum(a, b) + jnp.log1p(jnp.exp(-jnp.abs(a - b)))
        kk = (1.0 - lb) * _sigmoid(-z)
        if valid_len is not None:
            log_f = jnp.where(valid, log_f, 0.0)
            kk = jnp.where(valid, kk, 0.0)
        qs = qc * _sigmoid(qc)
        cum = _cumsum_rows(log_f)
        v_bf = vv.astype(BF16)
        st = st_ref[h]

        o = jnp.sum(qs * kk, axis=-1, keepdims=True) * vv
        for d in range(1, HG_SUB):
            w = jnp.exp(jnp.minimum(cum - pltpu.roll(cum, d, 0), 0.0))
            att = jnp.sum(qs * pltpu.roll(kk, d, 0) * w, axis=-1, keepdims=True)
            o = o + jnp.where(sub >= d, att, 0.0) * pltpu.roll(vv, d, 0)

        parts = [o[:HG_SUB]]
        for i in range(1, n // HG_SUB):
            lo, hi = i * HG_SUB, (i + 1) * HG_SUB
            ref_row = cum[lo - 1:lo]
            qi = qs[lo:hi] * jnp.exp(cum[lo:hi] - ref_row)
            kp = kk[:lo] * jnp.exp(ref_row - cum[:lo])
            att = lax.dot_general(qi.astype(BF16), kp.astype(BF16), (((1,), (1,)), ((), ())),
                                  preferred_element_type=F32)
            parts.append(o[lo:hi] + jnp.dot(att.astype(BF16), v_bf[:lo], preferred_element_type=F32))
        o = jnp.concatenate(parts, axis=0)

        o = o + lax.dot_general((qs * jnp.exp(cum)).astype(BF16), st.astype(BF16),
                                (((1,), (1,)), ((), ())), preferred_element_type=F32)
        last = cum[n - 1:n]
        kd = kk * jnp.exp(last - cum)
        st_new = st * jnp.exp(last) + lax.dot_general(v_bf, kd.astype(BF16), (((0,), (0,)), ((), ())),
                                                     preferred_element_type=F32)
        st_ref[h] = st_new

        ms = jnp.mean(o * o, axis=-1, keepdims=True)
        o_ref[:, hs] = o * lax.rsqrt(ms + EPS) * ng_ref[...]

        @pl.when(c == nc - 1)
        def _(h=h, st_new=st_new):
            sf_ref[0, h] = st_new.T


def _hgrn(proj, lb_param, norm_g, state0, layer, *, nb, seq, row0, valid_len):
    n = HG_CHUNK
    nc = seq // n
    rb0 = row0 // n
    has_state = state0 is not None
    depth = lb_param.shape[0]

    def col(cidx):
        return pl.BlockSpec((n, WIDTH), lambda b, c: (rb0 + b * nc + c, cidx))

    in_specs = [col(C_FC), col(C_QC), col(C_IC),
                pl.BlockSpec((depth, WIDTH), lambda b, c: (0, 0)),
                pl.BlockSpec((1, HEAD_DIM), lambda b, c: (0, 0))]
    args = [proj, proj, proj, lb_param, norm_g.reshape(1, HEAD_DIM)]
    if has_state:
        in_specs.append(pl.BlockSpec((1, HEADS, HEAD_DIM, HEAD_DIM), lambda b, c: (b, 0, 0, 0)))
        args.append(state0)
    return pl.pallas_call(
        functools.partial(_hgrn_kernel, layer=layer, valid_len=valid_len, has_state=has_state),
        grid=(nb, nc),
        in_specs=in_specs,
        out_specs=[
            pl.BlockSpec((n, WIDTH), lambda b, c: (b * nc + c, 0)),
            pl.BlockSpec((1, HEADS, HEAD_DIM, HEAD_DIM), lambda b, c: (b, 0, 0, 0)),
        ],
        out_shape=[
            jax.ShapeDtypeStruct((nb * seq, WIDTH), F32),
            jax.ShapeDtypeStruct((nb, HEADS, HEAD_DIM, HEAD_DIM), F32),
        ],
        scratch_shapes=[pltpu.VMEM((HEADS, HEAD_DIM, HEAD_DIM), F32)],
        compiler_params=_params("parallel", "arbitrary"),
        name="hgrn",
    )(*args)


def _merge_kernel(h_ref, ya_ref, yb_ref, yc_ref, ga_ref, gb_ref, gc_ref, ma_ref, mb_ref, mc_ref,
                  wb_ref, wo_ref, o_ref):
    merged = None
    for n, (y_ref, g_ref, m_ref) in enumerate(((ya_ref, ga_ref, ma_ref), (yb_ref, gb_ref, mb_ref),
                                                (yc_ref, gc_ref, mc_ref))):
        g = g_ref[...]
        y = (y_ref[...] * (g * _sigmoid(g))).astype(BF16)
        p = _sigmoid(m_ref[...]) * jnp.dot(y, wb_ref[n], preferred_element_type=F32)
        merged = p if merged is None else merged + p
    o_ref[...] = h_ref[...] + jnp.dot(merged.astype(BF16), wo_ref[...], preferred_element_type=F32)


def _merge(h_all, ya, yb, yc, proj, wb_bf, wo_bf):
    t = h_all.shape[0]
    tm = _pick_tile(t, (256, 128))
    row = pl.BlockSpec((tm, WIDTH), lambda i: (i, 0))

    def col(cidx):
        return pl.BlockSpec((tm, WIDTH), lambda i: (i, cidx))

    return pl.pallas_call(
        _merge_kernel,
        grid=(t // tm,),
        in_specs=[row, row, row, row, col(C_GA), col(C_GB), col(C_GC), col(C_MA), col(C_MB), col(C_MC),
                  pl.BlockSpec((3, WIDTH, D_MODEL), lambda i: (0, 0, 0)),
                  pl.BlockSpec((D_MODEL, D_MODEL), lambda i: (0, 0))],
        out_specs=row,
        out_shape=jax.ShapeDtypeStruct((t, D_MODEL), F32),
        compiler_params=_params("parallel"),
        name="merge",
    )(h_all, ya, yb, yc, proj, proj, proj, proj, proj, proj, wb_bf, wo_bf)


def kernel(x_prompt, x_sample, cache_k, cache_v, state_pool, state_hgrn, meta_tokens, norm_g, w_in,
           q_norm_g, k_norm_g, w_pool, pool_scale, hgrn_lower_bounds, hgrn_norm_g, w_branch, w_out):
    bp, seq, _ = x_prompt.shape
    bs, dseq, _ = x_sample.shape
    depth = w_in.shape[0]
    past = cache_k.shape[2]
    lp_real = N_META + seq
    lp = -(-lp_real // LANE) * LANE
    tp = bp * lp
    ts = bs * dseq

    meta = jnp.broadcast_to(meta_tokens.astype(F32)[None], (bp, N_META, D_MODEL))
    hp = jnp.concatenate([meta, x_prompt, jnp.zeros((bp, lp - lp_real, D_MODEL), F32)], axis=1)
    h_all = jnp.concatenate([hp.reshape(tp, D_MODEL), x_sample.reshape(ts, D_MODEL)], axis=0)

    cache_k2 = cache_k.reshape(depth, bs, past, WIDTH)
    cache_v2 = cache_v.reshape(depth, bs, past, WIDTH)
    prefix16 = jnp.pad(state_pool, ((0, 0), (0, 0), (1, 0), (0, 0)))

    kp, vp, pp, sp, ksm, vsm, psm, ssm = [], [], [], [], [], [], [], []
    for l in range(depth):
        proj = _in_proj(h_all, norm_g[l], w_in[l].astype(BF16), q_norm_g[l], k_norm_g[l])
        w_pool_bf = w_pool[l].astype(BF16)

        def cols(cidx, r0, r1):
            return proj[r0:r1, cidx * WIDTH:(cidx + 1) * WIDTH]

        ya_p = _pool(proj, None, w_pool_bf, pool_scale[l], nb=bp, seq=lp, row0=0, pos0=0)
        yb_p = _attn_prompt(proj, cols(C_KB, 0, tp).astype(BF16), cols(C_VB, 0, tp).astype(BF16),
                            nb=bp, seq=lp)
        yc_p, st_p = _hgrn(proj, hgrn_lower_bounds, hgrn_norm_g[l], None, l,
                           nb=bp, seq=lp, row0=0, valid_len=lp_real)
        ya_s = _pool(proj, prefix16[l], w_pool_bf, pool_scale[l], nb=bs, seq=dseq, row0=tp, pos0=past)
        yb_s = _attn_sample(proj, cache_k2, cache_v2, l, nb=bs, seq=dseq, row0=tp)
        yc_s, st_s = _hgrn(proj, hgrn_lower_bounds, hgrn_norm_g[l], state_hgrn[l], l,
                           nb=bs, seq=dseq, row0=tp, valid_len=None)

        ya = jnp.concatenate([ya_p, ya_s], axis=0)
        yb = jnp.concatenate([yb_p, yb_s], axis=0)
        yc = jnp.concatenate([yc_p, yc_s], axis=0)
        h_all = _merge(h_all, ya, yb, yc, proj, w_branch[l].astype(BF16), w_out[l].astype(BF16))

        def prompt_rows(cidx, r0, r1):
            return cols(cidx, 0, tp).reshape(bp, lp, WIDTH)[:, r0:r1]

        kp.append(prompt_rows(C_KB, 0, lp_real).reshape(bp, lp_real, HEADS, HEAD_DIM))
        vp.append(prompt_rows(C_VB, 0, lp_real).reshape(bp, lp_real, HEADS, HEAD_DIM))
        pp.append(prompt_rows(C_UA, lp_real - (POOL_MAX - 1), lp_real))
        sp.append(st_p)
        ksm.append(cols(C_KB, tp, tp + ts).reshape(bs, dseq, HEADS, HEAD_DIM))
        vsm.append(cols(C_VB, tp, tp + ts).reshape(bs, dseq, HEADS, HEAD_DIM))
        full_s = jnp.concatenate([state_pool[l], cols(C_UA, tp, tp + ts).reshape(bs, dseq, WIDTH)], axis=1)
        psm.append(full_s[:, -(POOL_MAX - 1):])
        ssm.append(st_s)

    y_prompt = h_all[:tp].reshape(bp, lp, D_MODEL)[:, N_META:lp_real]
    y_sample = h_all[tp:].reshape(bs, dseq, D_MODEL)
    return (y_prompt, y_sample, jnp.stack(kp), jnp.stack(vp), jnp.stack(pp), jnp.stack(sp),
            jnp.stack(ksm), jnp.stack(vsm), jnp.stack(psm), jnp.stack(ssm))
```

```python
import functools

import jax
import jax.numpy as jnp
from jax import lax
from jax.experimental import pallas as pl
from jax.experimental.pallas import tpu as pltpu

F32 = jnp.float32
BF16 = jnp.bfloat16

D_MODEL = 1024
WIDTH = 1024
N_SPLIT = 13
N_META = 16
POOL_WINDOWS = (2, 4, 8, 16)
POOL_GW = WIDTH // len(POOL_WINDOWS)
POOL_MAX = max(POOL_WINDOWS)
HEADS = 8
HEAD_DIM = WIDTH // HEADS
SB_SCALE = HEAD_DIM ** -0.5
LOG2E = 1.4426950408889634
EPS = 1e-6
LB_FLOOR = 1e-30
HG_CHUNK = 64
HG_SUB = 8
LANE = 128
TK_WIDE = 256
VMEM_LIMIT = 56 * 1024 * 1024

C_UA, C_GA, C_QB, C_KB, C_VB, C_GB, C_FC, C_QC, C_IC, C_GC, C_MA, C_MB, C_MC = range(N_SPLIT)


def _sigmoid(x):
    return 1.0 / (1.0 + jnp.exp(-x))


def _log1pexp_neg_abs(x):
    return jnp.log(1.0 + jnp.exp(-jnp.abs(x)))


def _softplus(x):
    return jnp.maximum(x, 0.0) + _log1pexp_neg_abs(x)


def _params(*sem):
    return pltpu.CompilerParams(dimension_semantics=sem, vmem_limit_bytes=VMEM_LIMIT)


def _in_proj_kernel(x_ref, g_ref, w_ref, qg_ref, kg_ref, o_ref, xn_ref):
    j = pl.program_id(1)

    @pl.when(j == 0)
    def _():
        x = x_ref[...]
        ms = jnp.mean(x * x, axis=-1, keepdims=True)
        xn_ref[...] = (x * lax.rsqrt(ms + EPS) * g_ref[...]).astype(BF16)

    acc = jnp.dot(xn_ref[...], w_ref[...], preferred_element_type=F32)
    is_q = j == C_QB
    is_qk = jnp.logical_or(is_q, j == C_KB)

    @pl.when(is_qk)
    def _():
        gain = jnp.where(is_q, qg_ref[...], kg_ref[...])
        for h in range(HEADS):
            hs = slice(h * HEAD_DIM, (h + 1) * HEAD_DIM)
            a = acc[:, hs]
            ms = jnp.mean(a * a, axis=-1, keepdims=True)
            o_ref[:, hs] = a * lax.rsqrt(ms + EPS) * gain

    @pl.when(jnp.logical_not(is_qk))
    def _():
        o_ref[...] = acc


def _pick_tile(t, candidates):
    return next(c for c in candidates if t % c == 0)


def _in_proj(h_all, norm_g, w_bf, q_g, k_g):
    t = h_all.shape[0]
    tm = _pick_tile(t, (1024, 512, 256, 128))
    return pl.pallas_call(
        _in_proj_kernel,
        grid=(t // tm, N_SPLIT),
        in_specs=[
            pl.BlockSpec((tm, D_MODEL), lambda i, j: (i, 0)),
            pl.BlockSpec((1, D_MODEL), lambda i, j: (0, 0)),
            pl.BlockSpec((D_MODEL, WIDTH), lambda i, j: (0, j)),
            pl.BlockSpec((1, HEAD_DIM), lambda i, j: (0, 0)),
            pl.BlockSpec((1, HEAD_DIM), lambda i, j: (0, 0)),
        ],
        out_specs=pl.BlockSpec((tm, WIDTH), lambda i, j: (i, j)),
        out_shape=jax.ShapeDtypeStruct((t, N_SPLIT * WIDTH), F32),
        scratch_shapes=[pltpu.VMEM((tm, D_MODEL), BF16)],
        compiler_params=_params("parallel", "arbitrary"),
        name="in_proj",
    )(h_all, norm_g.reshape(1, D_MODEL), w_bf, q_g.reshape(1, HEAD_DIM), k_g.reshape(1, HEAD_DIM))


def _pool_kernel(*refs, seq, pos0, has_prefix):
    if has_prefix:
        u_ref, p_ref, w_ref, s_ref, o_ref = refs
    else:
        u_ref, w_ref, s_ref, o_ref = refs
    g = pl.program_id(1)
    u = u_ref[...]
    head = p_ref[0] if has_prefix else jnp.zeros((POOL_MAX, POOL_GW), F32)
    x = jnp.concatenate([head, u], axis=0)
    pos = pos0 + lax.broadcasted_iota(jnp.int32, (seq, 1), 0)

    for gi, win in enumerate(POOL_WINDOWS):
        @pl.when(g == gi)
        def _(win=win):
            s = x
            sh = 1
            while sh < win:
                s = s + pltpu.roll(s, sh, 0)
                sh *= 2
            cnt = jnp.minimum(pos + 1, win).astype(F32)
            diff = s[POOL_MAX:] / cnt - u
            y = jnp.dot(diff.astype(BF16), w_ref[0], preferred_element_type=F32)
            o_ref[...] = y * s_ref[...]


def _pool(proj, prefix16, w_pool_bf, scale, *, nb, seq, row0, pos0):
    t = proj.shape[0]
    rb0 = row0 // seq
    has_prefix = prefix16 is not None
    in_specs = [pl.BlockSpec((seq, POOL_GW), lambda b, g: (rb0 + b, g))]
    args = [proj]
    if has_prefix:
        in_specs.append(pl.BlockSpec((1, POOL_MAX, POOL_GW), lambda b, g: (b, 0, g)))
        args.append(prefix16)
    in_specs += [
        pl.BlockSpec((1, POOL_GW, POOL_GW), lambda b, g: (g, 0, 0)),
        pl.BlockSpec((1, POOL_GW), lambda b, g: (0, g)),
    ]
    args += [w_pool_bf, scale.reshape(1, WIDTH)]
    return pl.pallas_call(
        functools.partial(_pool_kernel, seq=seq, pos0=pos0, has_prefix=has_prefix),
        grid=(nb, len(POOL_WINDOWS)),
        in_specs=in_specs,
        out_specs=pl.BlockSpec((seq, POOL_GW), lambda b, g: (b, g)),
        out_shape=jax.ShapeDtypeStruct((nb * seq, WIDTH), F32),
        compiler_params=_params("parallel", "parallel"),
        name="pool",
    )(*args)


def _tri_after(n):
    r = lax.broadcasted_iota(jnp.int32, (n, n), 0)
    c = lax.broadcasted_iota(jnp.int32, (n, n), 1)
    return jnp.where(r > c, 1.0, 0.0).astype(BF16)


def _head(h):
    return slice(h * HEAD_DIM, (h + 1) * HEAD_DIM)


def _sb_heads(qs_ref, key, val, tk, tri, carry_ref, keep_ref, logit_ref, o_ref, first):
    tq = qs_ref.shape[0]
    if first:
        r = lax.broadcasted_iota(jnp.int32, (tq, tk), 0)
        c = lax.broadcasted_iota(jnp.int32, (tq, tk), 1)
        mask = c < r
    for h in range(HEADS):
        rows = slice(h * tq, (h + 1) * tq)
        z = lax.dot_general(qs_ref[:, _head(h)], key(h), (((1,), (1,)), ((), ())),
                            preferred_element_type=F32)
        soft = jnp.maximum(z, 0.0) + jnp.log2(1.0 + jnp.exp2(-jnp.abs(z)))
        keep = -soft
        if first:
            keep = jnp.where(mask, keep, 0.0)
            logit_ref[rows, :tk] = z - soft
            carry_ref[h] = jnp.sum(keep, axis=1, keepdims=True)
        else:
            carry = carry_ref[h]
            logit_ref[rows, :tk] = z - soft + carry
            carry_ref[h] = carry + jnp.sum(keep, axis=1, keepdims=True)
        keep_ref[rows, :tk] = keep.astype(BF16)
    later = jnp.dot(keep_ref[:, :tk], tri, preferred_element_type=F32)
    for h in range(HEADS):
        rows = slice(h * tq, (h + 1) * tq)
        wts = jnp.exp2(logit_ref[rows, :tk] + later[rows])
        if first:
            wts = jnp.where(mask, wts, 0.0)
        pv = jnp.dot(wts.astype(BF16), val(h), preferred_element_type=F32)
        if first:
            o_ref[:, _head(h)] = pv
        else:
            o_ref[:, _head(h)] += pv


def _attn_scratch(tq):
    return [pltpu.VMEM((tq, WIDTH), BF16),
            pltpu.VMEM((TK_WIDE, TK_WIDE), BF16),
            pltpu.VMEM((HEADS, tq, 1), F32),
            pltpu.VMEM((HEADS * tq, TK_WIDE), BF16),
            pltpu.VMEM((HEADS * tq, TK_WIDE), F32)]


def _attn_prompt_kernel(q_ref, k_ref, v_ref, o_ref, qs_ref, triw_ref, carry_ref, keep_ref, logit_ref):
    i = pl.program_id(1)
    tq = q_ref.shape[0]
    qs_ref[...] = (q_ref[...] * (SB_SCALE * LOG2E)).astype(BF16)
    triw_ref[...] = _tri_after(TK_WIDE)
    d0 = pl.multiple_of(i * tq, tq)

    def sweep(k0, tk, tri, first):
        _sb_heads(qs_ref, lambda h: k_ref[pl.ds(k0, tk), _head(h)], lambda h: v_ref[pl.ds(k0, tk), _head(h)],
                  tk, tri, carry_ref, keep_ref, logit_ref, o_ref, first)

    sweep(d0, tq, _tri_after(tq), True)

    @pl.loop(0, lax.shift_right_logical(i, 1))
    def _(t):
        sweep(pl.multiple_of(d0 - (t + 1) * TK_WIDE, tq), TK_WIDE, triw_ref[...], False)

    @pl.when(jnp.bitwise_and(i, 1) == 1)
    def _():
        sweep(0, tq, _tri_after(tq), False)


def _attn_prompt(proj, k_bf, v_bf, *, nb, seq):
    nq = seq // LANE
    return pl.pallas_call(
        _attn_prompt_kernel,
        grid=(nb, nq),
        in_specs=[
            pl.BlockSpec((LANE, WIDTH), lambda b, i: (b * nq + i, C_QB)),
            pl.BlockSpec((seq, WIDTH), lambda b, i: (b, 0)),
            pl.BlockSpec((seq, WIDTH), lambda b, i: (b, 0)),
        ],
        out_specs=pl.BlockSpec((LANE, WIDTH), lambda b, i: (b * nq + i, 0)),
        out_shape=jax.ShapeDtypeStruct((nb * seq, WIDTH), F32),
        scratch_shapes=_attn_scratch(LANE),
        compiler_params=_params("parallel", "arbitrary"),
        name="attn_prompt",
    )(proj, k_bf, v_bf)


def _attn_sample_kernel(q_ref, kn_ref, vn_ref, kp_ref, vp_ref, o_ref, qs_ref, triw_ref, carry_ref, keep_ref,
                        logit_ref, *, past):
    tq = q_ref.shape[0]
    qs_ref[...] = (q_ref[...] * (SB_SCALE * LOG2E)).astype(BF16)
    triw_ref[...] = _tri_after(TK_WIDE)
    n_past = past // TK_WIDE

    _sb_heads(qs_ref, lambda h: kn_ref[:, _head(h)].astype(BF16), lambda h: vn_ref[:, _head(h)].astype(BF16),
              tq, _tri_after(tq), carry_ref, keep_ref, logit_ref, o_ref, True)

    @pl.loop(0, n_past)
    def _(t):
        k0 = (n_past - 1 - t) * TK_WIDE

        def head_rows(ref, h):
            rows = pl.ds(pl.multiple_of(k0 * HEADS, HEADS) + h, TK_WIDE, stride=HEADS)
            return ref[0, 0, rows, :].astype(BF16)

        _sb_heads(qs_ref, lambda h: head_rows(kp_ref, h), lambda h: head_rows(vp_ref, h),
                  TK_WIDE, triw_ref[...], carry_ref, keep_ref, logit_ref, o_ref, False)


def _attn_sample(proj, cache_k, cache_v, layer, *, nb, seq, row0):
    rows = cache_k.shape[2]
    rb0 = row0 // seq
    return pl.pallas_call(
        functools.partial(_attn_sample_kernel, past=rows // HEADS),
        grid=(nb,),
        in_specs=[
            pl.BlockSpec((seq, WIDTH), lambda b: (rb0 + b, C_QB)),
            pl.BlockSpec((seq, WIDTH), lambda b: (rb0 + b, C_KB)),
            pl.BlockSpec((seq, WIDTH), lambda b: (rb0 + b, C_VB)),
            pl.BlockSpec((1, 1, rows, HEAD_DIM), lambda b: (layer, b, 0, 0)),
            pl.BlockSpec((1, 1, rows, HEAD_DIM), lambda b: (layer, b, 0, 0)),
        ],
        out_specs=pl.BlockSpec((seq, WIDTH), lambda b: (b, 0)),
        out_shape=jax.ShapeDtypeStruct((nb * seq, WIDTH), F32),
        scratch_shapes=_attn_scratch(seq),
        compiler_params=_params("parallel"),
        name="attn_sample",
    )(proj, proj, proj, cache_k, cache_v)


def _group_roll(x, d):
    n, lanes = x.shape
    return pltpu.roll(x.reshape(n // HG_SUB, HG_SUB, lanes), d, 1).reshape(n, lanes)


def _cumsum_rows(x, sub):
    n = x.shape[0]
    sh = 1
    while sh < HG_SUB:
        x = x + jnp.where(sub >= sh, _group_roll(x, sh), 0.0)
        sh *= 2
    parts, run = [], None
    for g in range(n // HG_SUB):
        blk = x[g * HG_SUB:(g + 1) * HG_SUB]
        if run is not None:
            blk = blk + run
        parts.append(blk)
        run = blk[HG_SUB - 1:HG_SUB]
    return jnp.concatenate(parts, axis=0)


def _pair_classes(n):
    r = lax.broadcasted_iota(jnp.int32, (n, n), 0)
    c = lax.broadcasted_iota(jnp.int32, (n, n), 1)

    def block_of(x, size):
        return jnp.bitwise_and(x, -size)

    same = block_of(r, HG_SUB) == block_of(c, HG_SUB)
    delta = jnp.where(jnp.logical_and(same, c <= r), r - c, -1)
    level = jnp.full((n, n), -1, jnp.int32)
    half, m = HG_SUB, 0
    while half < n:
        blk = 2 * half
        hit = jnp.logical_and(block_of(r, blk) == block_of(c, blk),
                              jnp.logical_and(jnp.bitwise_and(r, half) != 0, jnp.bitwise_and(c, half) == 0))
        level = jnp.where(hit, m, level)
        half, m = blk, m + 1
    return delta, level


def _hgrn_kernel(*refs, layer, valid_len, has_state):
    if has_state:
        z_ref, q_ref, v_ref, lbp_ref, ng_ref, s0_ref, o_ref, sf_ref, st_ref = refs
    else:
        z_ref, q_ref, v_ref, lbp_ref, ng_ref, o_ref, sf_ref, st_ref = refs
    c = pl.program_id(1)
    nc = pl.num_programs(1)
    n = HG_CHUNK

    lbp = lbp_ref[...]
    e = jnp.exp(lbp - jnp.max(lbp, axis=0, keepdims=True))
    soft = e / jnp.sum(e, axis=0, keepdims=True)
    run = soft[0:1]
    for l in range(1, layer + 1):
        run = run + soft[l:l + 1]
    lb_all = run - soft[0:1]

    row = lax.broadcasted_iota(jnp.int32, (n, 1), 0)
    valid = None if valid_len is None else (c * n + row) < valid_len
    sub = jnp.bitwise_and(row, HG_SUB - 1)
    delta, level = _pair_classes(n)

    @pl.when(c == 0)
    def _():
        for h in range(HEADS):
            st_ref[h] = s0_ref[0, h].T if has_state else jnp.zeros((HEAD_DIM, HEAD_DIM), F32)

    for h in range(HEADS):
        hs = _head(h)
        z = z_ref[:, hs]
        qc = q_ref[:, hs]
        vv = v_ref[:, hs]
        lb = lb_all[:, hs]
        ez = jnp.exp(-jnp.abs(z))
        a = jnp.log(jnp.maximum(lb, LB_FLOOR))
        b = jnp.log1p(-lb) + (jnp.minimum(z, 0.0) - jnp.log(1.0 + ez))
        log_f = jnp.maximum(a, b) + _log1pexp_neg_abs(a - b)
        kk = (1.0 - lb) * (jnp.where(z >= 0.0, ez, 1.0) / (1.0 + ez))
        if valid_len is not None:
            log_f = jnp.where(valid, log_f, 0.0)
            kk = jnp.where(valid, kk, 0.0)
        qs = qc * _sigmoid(qc)
        cum = _cumsum_rows(log_f, sub)
        v_bf = vv.astype(BF16)
        st = st_ref[h]

        att = jnp.where(delta == 0, jnp.sum(qs * kk, axis=-1, keepdims=True), 0.0)
        for d in range(1, HG_SUB):
            w = jnp.exp(jnp.minimum(cum - _group_roll(cum, d), 0.0))
            a_d = jnp.sum(qs * _group_roll(kk, d) * w, axis=-1, keepdims=True)
            att = jnp.where(delta == d, a_d, att)

        half, m = HG_SUB, 0
        while half < n:
            blk = 2 * half
            ref = jnp.concatenate([jnp.broadcast_to(cum[s + half - 1:s + half], (blk, HEAD_DIM))
                                   for s in range(0, n, blk)], axis=0)
            second = jnp.bitwise_and(row, half) != 0
            x = (jnp.where(second, qs, kk) * jnp.exp(-jnp.abs(cum - ref))).astype(BF16)
            a_m = lax.dot_general(x, x, (((1,), (1,)), ((), ())), preferred_element_type=F32)
            att = jnp.where(level == m, a_m, att)
            half, m = blk, m + 1

        o = jnp.dot(att.astype(BF16), v_bf, preferred_element_type=F32)
        o = o + lax.dot_general((qs * jnp.exp(cum)).astype(BF16), st.astype(BF16),
                                (((1,), (1,)), ((), ())), preferred_element_type=F32)
        last = cum[n - 1:n]
        kd = kk * jnp.exp(last - cum)
        st_new = st * jnp.exp(last) + lax.dot_general(v_bf, kd.astype(BF16), (((0,), (0,)), ((), ())),
                                                     preferred_element_type=F32)
        st_ref[h] = st_new

        ms = jnp.mean(o * o, axis=-1, keepdims=True)
        o_ref[:, hs] = o * lax.rsqrt(ms + EPS) * ng_ref[...]

    @pl.when(c == nc - 1)
    def _():
        for h in range(HEADS):
            sf_ref[0, h] = st_ref[h].T


def _hgrn(proj, lb_param, norm_g, state0, layer, *, nb, seq, row0, valid_len):
    n = HG_CHUNK
    nc = seq // n
    rb0 = row0 // n
    has_state = state0 is not None
    depth = lb_param.shape[0]

    def col(cidx):
        return pl.BlockSpec((n, WIDTH), lambda b, c: (rb0 + b * nc + c, cidx))

    in_specs = [col(C_FC), col(C_QC), col(C_IC),
                pl.BlockSpec((depth, WIDTH), lambda b, c: (0, 0)),
                pl.BlockSpec((1, HEAD_DIM), lambda b, c: (0, 0))]
    args = [proj, proj, proj, lb_param, norm_g.reshape(1, HEAD_DIM)]
    if has_state:
        in_specs.append(pl.BlockSpec((1, HEADS, HEAD_DIM, HEAD_DIM), lambda b, c: (b, 0, 0, 0)))
        args.append(state0)
    return pl.pallas_call(
        functools.partial(_hgrn_kernel, layer=layer, valid_len=valid_len, has_state=has_state),
        grid=(nb, nc),
        in_specs=in_specs,
        out_specs=[
            pl.BlockSpec((n, WIDTH), lambda b, c: (b * nc + c, 0)),
            pl.BlockSpec((1, HEADS, HEAD_DIM, HEAD_DIM), lambda b, c: (b, 0, 0, 0)),
        ],
        out_shape=[
            jax.ShapeDtypeStruct((nb * seq, WIDTH), F32),
            jax.ShapeDtypeStruct((nb, HEADS, HEAD_DIM, HEAD_DIM), F32),
        ],
        scratch_shapes=[pltpu.VMEM((HEADS, HEAD_DIM, HEAD_DIM), F32)],
        compiler_params=_params("parallel", "arbitrary"),
        name="hgrn",
    )(*args)


def _merge_kernel(h_ref, yap_ref, ybp_ref, ycp_ref, yas_ref, ybs_ref, ycs_ref, ga_ref, gb_ref, gc_ref,
                  ma_ref, mb_ref, mc_ref, wb_ref, wo_ref, o_ref, *, prompt_tiles):
    is_prompt = pl.program_id(0) < prompt_tiles
    merged = None
    for n, (yp_ref, ys_ref, g_ref, m_ref) in enumerate(((yap_ref, yas_ref, ga_ref, ma_ref),
                                                        (ybp_ref, ybs_ref, gb_ref, mb_ref),
                                                        (ycp_ref, ycs_ref, gc_ref, mc_ref))):
        g = g_ref[...]
        y = jnp.where(is_prompt, yp_ref[...], ys_ref[...])
        y = (y * (g * _sigmoid(g))).astype(BF16)
        p = _sigmoid(m_ref[...]) * jnp.dot(y, wb_ref[n], preferred_element_type=F32)
        merged = p if merged is None else merged + p
    o_ref[...] = h_ref[...] + jnp.dot(merged.astype(BF16), wo_ref[...], preferred_element_type=F32)


def _merge(h_all, y_prompt, y_sample, proj, wb_bf, wo_bf):
    t = h_all.shape[0]
    tp, ts = y_prompt[0].shape[0], y_sample[0].shape[0]
    tm = next(c for c in (256, 128) if tp % c == 0 and ts % c == 0)
    assert tp + ts == t
    n_p = tp // tm
    row = pl.BlockSpec((tm, WIDTH), lambda i: (i, 0))
    prow = pl.BlockSpec((tm, WIDTH), lambda i: (jnp.minimum(i, n_p - 1), 0))
    srow = pl.BlockSpec((tm, WIDTH), lambda i: (jnp.maximum(i - n_p, 0), 0))

    def col(cidx):
        return pl.BlockSpec((tm, WIDTH), lambda i: (i, cidx))

    return pl.pallas_call(
        functools.partial(_merge_kernel, prompt_tiles=n_p),
        grid=(t // tm,),
        in_specs=[row, prow, prow, prow, srow, srow, srow,
                  col(C_GA), col(C_GB), col(C_GC), col(C_MA), col(C_MB), col(C_MC),
                  pl.BlockSpec((3, WIDTH, D_MODEL), lambda i: (0, 0, 0)),
                  pl.BlockSpec((D_MODEL, D_MODEL), lambda i: (0, 0))],
        out_specs=row,
        out_shape=jax.ShapeDtypeStruct((t, D_MODEL), F32),
        compiler_params=_params("arbitrary"),
        name="merge",
    )(h_all, *y_prompt, *y_sample, proj, proj, proj, proj, proj, proj, wb_bf, wo_bf)


def kernel(x_prompt, x_sample, cache_k, cache_v, state_pool, state_hgrn, meta_tokens, norm_g, w_in,
           q_norm_g, k_norm_g, w_pool, pool_scale, hgrn_lower_bounds, hgrn_norm_g, w_branch, w_out):
    bp, seq, _ = x_prompt.shape
    bs, dseq, _ = x_sample.shape
    depth = w_in.shape[0]
    past = cache_k.shape[2]
    lp_real = N_META + seq
    lp = -(-lp_real // LANE) * LANE
    tp = bp * lp
    ts = bs * dseq

    meta = jnp.broadcast_to(meta_tokens.astype(F32)[None], (bp, N_META, D_MODEL))
    hp = jnp.concatenate([meta, x_prompt, jnp.zeros((bp, lp - lp_real, D_MODEL), F32)], axis=1)
    h_all = jnp.concatenate([hp.reshape(tp, D_MODEL), x_sample.reshape(ts, D_MODEL)], axis=0)

    cache_k2 = cache_k.reshape(depth, bs, past * HEADS, HEAD_DIM)
    cache_v2 = cache_v.reshape(depth, bs, past * HEADS, HEAD_DIM)
    prefix16 = jnp.pad(state_pool, ((0, 0), (0, 0), (1, 0), (0, 0)))

    kp, vp, pp, sp, ksm, vsm, psm, ssm = [], [], [], [], [], [], [], []
    for l in range(depth):
        proj = _in_proj(h_all, norm_g[l], w_in[l].astype(BF16), q_norm_g[l], k_norm_g[l])
        w_pool_bf = w_pool[l].astype(BF16)

        def cols(cidx, r0, r1):
            return proj[r0:r1, cidx * WIDTH:(cidx + 1) * WIDTH]

        ya_p = _pool(proj, None, w_pool_bf, pool_scale[l], nb=bp, seq=lp, row0=0, pos0=0)
        yb_p = _attn_prompt(proj, cols(C_KB, 0, tp).astype(BF16), cols(C_VB, 0, tp).astype(BF16),
                            nb=bp, seq=lp)
        yc_p, st_p = _hgrn(proj, hgrn_lower_bounds, hgrn_norm_g[l], None, l,
                           nb=bp, seq=lp, row0=0, valid_len=lp_real)
        ya_s = _pool(proj, prefix16[l], w_pool_bf, pool_scale[l], nb=bs, seq=dseq, row0=tp, pos0=past)
        yb_s = _attn_sample(proj, cache_k2, cache_v2, l, nb=bs, seq=dseq, row0=tp)
        yc_s, st_s = _hgrn(proj, hgrn_lower_bounds, hgrn_norm_g[l], state_hgrn[l], l,
                           nb=bs, seq=dseq, row0=tp, valid_len=None)

        h_all = _merge(h_all, (ya_p, yb_p, yc_p), (ya_s, yb_s, yc_s), proj,
                       w_branch[l].astype(BF16), w_out[l].astype(BF16))

        def prompt_rows(cidx, r0, r1):
            return cols(cidx, 0, tp).reshape(bp, lp, WIDTH)[:, r0:r1]

        kp.append(prompt_rows(C_KB, 0, lp_real).reshape(bp, lp_real, HEADS, HEAD_DIM))
        vp.append(prompt_rows(C_VB, 0, lp_real).reshape(bp, lp_real, HEADS, HEAD_DIM))
        pp.append(prompt_rows(C_UA, lp_real - (POOL_MAX - 1), lp_real))
        sp.append(st_p)
        ksm.append(cols(C_KB, tp, tp + ts).reshape(bs, dseq, HEADS, HEAD_DIM))
        vsm.append(cols(C_VB, tp, tp + ts).reshape(bs, dseq, HEADS, HEAD_DIM))
        full_s = jnp.concatenate([state_pool[l], cols(C_UA, tp, tp + ts).reshape(bs, dseq, WIDTH)], axis=1)
        psm.append(full_s[:, -(POOL_MAX - 1):])
        ssm.append(st_s)

    y_prompt = h_all[:tp].reshape(bp, lp, D_MODEL)[:, N_META:lp_real]
    y_sample = h_all[tp:].reshape(bs, dseq, D_MODEL)
    return (y_prompt, y_sample, jnp.stack(kp), jnp.stack(vp), jnp.stack(pp), jnp.stack(sp),
            jnp.stack(ksm), jnp.stack(vsm), jnp.stack(psm), jnp.stack(ssm))
```

```python
import functools

import jax
import jax.numpy as jnp
from jax import lax
from jax.experimental import pallas as pl
from jax.experimental.pallas import tpu as pltpu

F32 = jnp.float32
BF16 = jnp.bfloat16

D_MODEL = 1024
WIDTH = 1024
N_SPLIT = 13
N_META = 16
POOL_WINDOWS = (2, 4, 8, 16)
POOL_GW = WIDTH // len(POOL_WINDOWS)
POOL_MAX = max(POOL_WINDOWS)
HEADS = 8
HEAD_DIM = WIDTH // HEADS
SB_SCALE = HEAD_DIM ** -0.5
LOG2E = 1.4426950408889634
EPS = 1e-6
LB_FLOOR = 1e-30
HG_CHUNK = 64
HG_SUB = 8
LANE = 128
TK_WIDE = 256
VMEM_LIMIT = 56 * 1024 * 1024

C_UA, C_GA, C_QB, C_KB, C_VB, C_GB, C_FC, C_QC, C_IC, C_GC, C_MA, C_MB, C_MC = range(N_SPLIT)


def _head(h):
    return slice(h * HEAD_DIM, (h + 1) * HEAD_DIM)


def _sigmoid(x):
    return 1.0 / (1.0 + jnp.exp(-x))


def _log1pexp_neg_abs(x):
    return jnp.log(1.0 + jnp.exp(-jnp.abs(x)))


def _softplus(x):
    return jnp.maximum(x, 0.0) + _log1pexp_neg_abs(x)


def _params(*sem):
    return pltpu.CompilerParams(dimension_semantics=sem, vmem_limit_bytes=VMEM_LIMIT)


STEP_GROUPS = (C_UA, C_KB, C_VB, C_FC, C_QC, C_QB, C_IC, C_GA, C_GB, C_GC, C_MA, C_MB, C_MC)
P32_UA, P32_KB, P32_VB, P32_FC, P32_QC = range(5)
N_P32 = 5
P16_KB, P16_VB, P16_QB, P16_IC, P16_GA, P16_GB, P16_GC, P16_MA, P16_MB, P16_MC = range(10)
N_P16 = 10
STEP_P32 = (0, 1, 2, 3, 4, 4, 4, 4, 4, 4, 4, 4, 4)
STEP_P16 = (0, 0, 1, 1, 1, 2, 3, 4, 5, 6, 7, 8, 9)
S_UA, S_KB, S_VB, S_FC, S_QC, S_QB = range(6)


def _in_proj_kernel(tab_ref, x_ref, g_ref, w_ref, qg_ref, kg_ref, o32_ref, o16_ref, xn_ref):
    j = pl.program_id(1)

    @pl.when(j == 0)
    def _():
        x = x_ref[...]
        ms = jnp.mean(x * x, axis=-1, keepdims=True)
        xn_ref[...] = (x * lax.rsqrt(ms + EPS) * g_ref[...]).astype(BF16)

    def project():
        return jnp.dot(xn_ref[...], w_ref[...], preferred_element_type=F32)

    def head_norm(a, gain):
        ms = jnp.mean(a * a, axis=-1, keepdims=True)
        return a * lax.rsqrt(ms + EPS) * gain

    @pl.when(jnp.logical_or(j == S_UA, jnp.logical_or(j == S_FC, j == S_QC)))
    def _():
        o32_ref[...] = project()

    @pl.when(j == S_KB)
    def _():
        acc = project()
        for h in range(HEADS):
            kn = head_norm(acc[:, _head(h)], kg_ref[...])
            o32_ref[:, _head(h)] = kn
            o16_ref[:, _head(h)] = kn.astype(BF16)

    @pl.when(j == S_VB)
    def _():
        acc = project()
        o32_ref[...] = acc
        o16_ref[...] = acc.astype(BF16)

    @pl.when(j == S_QB)
    def _():
        acc = project()
        for h in range(HEADS):
            qn = head_norm(acc[:, _head(h)], qg_ref[...])
            o16_ref[:, _head(h)] = (qn * (SB_SCALE * LOG2E)).astype(BF16)

    @pl.when(j > S_QB)
    def _():
        o16_ref[...] = project().astype(BF16)


def _pick_tile(t, candidates):
    return next(c for c in candidates if t % c == 0)


def _in_proj(h_all, norm_g, w_bf, q_g, k_g):
    t = h_all.shape[0]
    tm = _pick_tile(t, (1024, 512, 256, 128))
    table = jnp.array([STEP_GROUPS, STEP_P32, STEP_P16], jnp.int32)
    grid_spec = pltpu.PrefetchScalarGridSpec(
        num_scalar_prefetch=1,
        grid=(t // tm, N_SPLIT),
        in_specs=[
            pl.BlockSpec((tm, D_MODEL), lambda i, j, tab: (i, 0)),
            pl.BlockSpec((1, D_MODEL), lambda i, j, tab: (0, 0)),
            pl.BlockSpec((D_MODEL, WIDTH), lambda i, j, tab: (0, tab[0, j])),
            pl.BlockSpec((1, HEAD_DIM), lambda i, j, tab: (0, 0)),
            pl.BlockSpec((1, HEAD_DIM), lambda i, j, tab: (0, 0)),
        ],
        out_specs=[
            pl.BlockSpec((tm, WIDTH), lambda i, j, tab: (i, tab[1, j])),
            pl.BlockSpec((tm, WIDTH), lambda i, j, tab: (i, tab[2, j])),
        ],
        scratch_shapes=[pltpu.VMEM((tm, D_MODEL), BF16)],
    )
    return pl.pallas_call(
        _in_proj_kernel,
        grid_spec=grid_spec,
        out_shape=[jax.ShapeDtypeStruct((t, N_P32 * WIDTH), F32),
                   jax.ShapeDtypeStruct((t, N_P16 * WIDTH), BF16)],
        compiler_params=_params("arbitrary", "arbitrary"),
        name="in_proj",
    )(table, h_all, norm_g.reshape(1, D_MODEL), w_bf, q_g.reshape(1, HEAD_DIM), k_g.reshape(1, HEAD_DIM))


def _pool_kernel(*refs, seq, pos0, has_prefix):
    if has_prefix:
        u_ref, p_ref, w_ref, s_ref, o_ref = refs
    else:
        u_ref, w_ref, s_ref, o_ref = refs
    g = pl.program_id(1)
    u = u_ref[...]
    head = p_ref[0] if has_prefix else jnp.zeros((POOL_MAX, POOL_GW), F32)
    x = jnp.concatenate([head, u], axis=0)
    pos = pos0 + lax.broadcasted_iota(jnp.int32, (seq, 1), 0)

    for gi, win in enumerate(POOL_WINDOWS):
        @pl.when(g == gi)
        def _(win=win):
            s = x
            sh = 1
            while sh < win:
                s = s + pltpu.roll(s, sh, 0)
                sh *= 2
            cnt = jnp.minimum(pos + 1, win).astype(F32)
            diff = s[POOL_MAX:] / cnt - u
            y = jnp.dot(diff.astype(BF16), w_ref[0], preferred_element_type=F32)
            o_ref[...] = (y * s_ref[...]).astype(o_ref.dtype)


def _pool(proj32, prefix16, w_pool_bf, scale, *, nb, seq, row0, pos0):
    rb0 = row0 // seq
    has_prefix = prefix16 is not None
    ua0 = P32_UA * len(POOL_WINDOWS)
    in_specs = [pl.BlockSpec((seq, POOL_GW), lambda b, g: (rb0 + b, ua0 + g))]
    args = [proj32]
    if has_prefix:
        in_specs.append(pl.BlockSpec((1, POOL_MAX, POOL_GW), lambda b, g: (b, 0, g)))
        args.append(prefix16)
    in_specs += [
        pl.BlockSpec((1, POOL_GW, POOL_GW), lambda b, g: (g, 0, 0)),
        pl.BlockSpec((1, POOL_GW), lambda b, g: (0, g)),
    ]
    args += [w_pool_bf, scale.reshape(1, WIDTH)]
    return pl.pallas_call(
        functools.partial(_pool_kernel, seq=seq, pos0=pos0, has_prefix=has_prefix),
        grid=(nb, len(POOL_WINDOWS)),
        in_specs=in_specs,
        out_specs=pl.BlockSpec((seq, POOL_GW), lambda b, g: (b, g)),
        out_shape=jax.ShapeDtypeStruct((nb * seq, WIDTH), BF16),
        compiler_params=_params("parallel", "parallel"),
        name="pool",
    )(*args)


def _tri_after(n):
    r = lax.broadcasted_iota(jnp.int32, (n, n), 0)
    c = lax.broadcasted_iota(jnp.int32, (n, n), 1)
    return jnp.where(r > c, 1.0, 0.0).astype(BF16)


def _sb_heads(qs_ref, key, val, tk, tri, carry_ref, keep_ref, logit_ref, o_ref, first):
    tq = qs_ref.shape[0]
    if first:
        r = lax.broadcasted_iota(jnp.int32, (tq, tk), 0)
        c = lax.broadcasted_iota(jnp.int32, (tq, tk), 1)
        mask = c < r
    for h in range(HEADS):
        rows = slice(h * tq, (h + 1) * tq)
        z = lax.dot_general(qs_ref[:, _head(h)], key(h), (((1,), (1,)), ((), ())),
                            preferred_element_type=F32)
        soft = jnp.maximum(z, 0.0) + jnp.log2(1.0 + jnp.exp2(-jnp.abs(z)))
        keep = -soft
        if first:
            keep = jnp.where(mask, keep, 0.0)
            logit_ref[rows, :tk] = z - soft
            carry_ref[h] = jnp.sum(keep, axis=1, keepdims=True)
        else:
            carry = carry_ref[h]
            logit_ref[rows, :tk] = z - soft + carry
            carry_ref[h] = carry + jnp.sum(keep, axis=1, keepdims=True)
        keep_ref[rows, :tk] = keep.astype(BF16)
    later = jnp.dot(keep_ref[:, :tk], tri, preferred_element_type=F32)
    for h in range(HEADS):
        rows = slice(h * tq, (h + 1) * tq)
        wts = jnp.exp2(logit_ref[rows, :tk] + later[rows])
        if first:
            wts = jnp.where(mask, wts, 0.0)
        pv = jnp.dot(wts.astype(BF16), val(h), preferred_element_type=F32)
        if first:
            o_ref[:, _head(h)] = pv
        else:
            o_ref[:, _head(h)] += pv


def _attn_scratch(tq):
    return [pltpu.VMEM((tq, WIDTH), F32),
            pltpu.VMEM((TK_WIDE, TK_WIDE), BF16),
            pltpu.VMEM((HEADS, tq, 1), F32),
            pltpu.VMEM((HEADS * tq, TK_WIDE), BF16),
            pltpu.VMEM((HEADS * tq, TK_WIDE), F32)]


def _attn_prompt_kernel(q_ref, k_ref, v_ref, o_ref, acc_ref, triw_ref, carry_ref, keep_ref, logit_ref):
    i = pl.program_id(1)
    tq = q_ref.shape[0]
    triw_ref[...] = _tri_after(TK_WIDE)
    d0 = pl.multiple_of(i * tq, tq)

    def sweep(k0, tk, tri, first):
        _sb_heads(q_ref, lambda h: k_ref[pl.ds(k0, tk), _head(h)], lambda h: v_ref[pl.ds(k0, tk), _head(h)],
                  tk, tri, carry_ref, keep_ref, logit_ref, acc_ref, first)

    sweep(d0, tq, _tri_after(tq), True)

    @pl.loop(0, lax.shift_right_logical(i, 1))
    def _(t):
        sweep(pl.multiple_of(d0 - (t + 1) * TK_WIDE, tq), TK_WIDE, triw_ref[...], False)

    @pl.when(jnp.bitwise_and(i, 1) == 1)
    def _():
        sweep(0, tq, _tri_after(tq), False)

    o_ref[...] = acc_ref[...].astype(o_ref.dtype)


def _attn_prompt(proj16, *, nb, seq):
    nq = seq // LANE
    return pl.pallas_call(
        _attn_prompt_kernel,
        grid=(nb, nq),
        in_specs=[
            pl.BlockSpec((LANE, WIDTH), lambda b, i: (b * nq + i, P16_QB)),
            pl.BlockSpec((seq, WIDTH), lambda b, i: (b, P16_KB)),
            pl.BlockSpec((seq, WIDTH), lambda b, i: (b, P16_VB)),
        ],
        out_specs=pl.BlockSpec((LANE, WIDTH), lambda b, i: (b * nq + i, 0)),
        out_shape=jax.ShapeDtypeStruct((nb * seq, WIDTH), BF16),
        scratch_shapes=_attn_scratch(LANE),
        compiler_params=_params("parallel", "arbitrary"),
        name="attn_prompt",
    )(proj16, proj16, proj16)


def _attn_sample_kernel(q_ref, kn_ref, vn_ref, kp_ref, vp_ref, o_ref, acc_ref, triw_ref, carry_ref, keep_ref,
                        logit_ref, *, past):
    tq = q_ref.shape[0]
    triw_ref[...] = _tri_after(TK_WIDE)
    n_past = past // TK_WIDE

    _sb_heads(q_ref, lambda h: kn_ref[:, _head(h)], lambda h: vn_ref[:, _head(h)],
              tq, _tri_after(tq), carry_ref, keep_ref, logit_ref, acc_ref, True)

    @pl.loop(0, n_past)
    def _(t):
        k0 = (n_past - 1 - t) * TK_WIDE

        def head_rows(ref, h):
            rows = pl.ds(pl.multiple_of(k0 * HEADS, HEADS) + h, TK_WIDE, stride=HEADS)
            return ref[0, 0, rows, :].astype(BF16)

        _sb_heads(q_ref, lambda h: head_rows(kp_ref, h), lambda h: head_rows(vp_ref, h),
                  TK_WIDE, triw_ref[...], carry_ref, keep_ref, logit_ref, acc_ref, False)

    o_ref[...] = acc_ref[...].astype(o_ref.dtype)


def _attn_sample(proj16, cache_k, cache_v, layer, *, nb, seq, row0):
    rows = cache_k.shape[2]
    rb0 = row0 // seq
    return pl.pallas_call(
        functools.partial(_attn_sample_kernel, past=rows // HEADS),
        grid=(nb,),
        in_specs=[
            pl.BlockSpec((seq, WIDTH), lambda b: (rb0 + b, P16_QB)),
            pl.BlockSpec((seq, WIDTH), lambda b: (rb0 + b, P16_KB)),
            pl.BlockSpec((seq, WIDTH), lambda b: (rb0 + b, P16_VB)),
            pl.BlockSpec((1, 1, rows, HEAD_DIM), lambda b: (layer, b, 0, 0)),
            pl.BlockSpec((1, 1, rows, HEAD_DIM), lambda b: (layer, b, 0, 0)),
        ],
        out_specs=pl.BlockSpec((seq, WIDTH), lambda b: (b, 0)),
        out_shape=jax.ShapeDtypeStruct((nb * seq, WIDTH), BF16),
        scratch_shapes=_attn_scratch(seq),
        compiler_params=_params("parallel"),
        name="attn_sample",
    )(proj16, proj16, proj16, cache_k, cache_v)


def _group_roll(x, d):
    n, lanes = x.shape
    return pltpu.roll(x.reshape(n // HG_SUB, HG_SUB, lanes), d, 1).reshape(n, lanes)


def _cumsum_rows(x, sub):
    n = x.shape[0]
    sh = 1
    while sh < HG_SUB:
        x = x + jnp.where(sub >= sh, _group_roll(x, sh), 0.0)
        sh *= 2
    parts, run = [], None
    for g in range(n // HG_SUB):
        blk = x[g * HG_SUB:(g + 1) * HG_SUB]
        if run is not None:
            blk = blk + run
        parts.append(blk)
        run = blk[HG_SUB - 1:HG_SUB]
    return jnp.concatenate(parts, axis=0)


def _pair_classes(n):
    r = lax.broadcasted_iota(jnp.int32, (n, n), 0)
    c = lax.broadcasted_iota(jnp.int32, (n, n), 1)

    def block_of(x, size):
        return jnp.bitwise_and(x, -size)

    same = block_of(r, HG_SUB) == block_of(c, HG_SUB)
    delta = jnp.where(jnp.logical_and(same, c <= r), r - c, -1)
    level = jnp.full((n, n), -1, jnp.int32)
    half, m = HG_SUB, 0
    while half < n:
        blk = 2 * half
        hit = jnp.logical_and(block_of(r, blk) == block_of(c, blk),
                              jnp.logical_and(jnp.bitwise_and(r, half) != 0, jnp.bitwise_and(c, half) == 0))
        level = jnp.where(hit, m, level)
        half, m = blk, m + 1
    return delta, level


def _hgrn_kernel(*refs, layer, valid_len, has_state):
    if has_state:
        z_ref, q_ref, v_ref, lbp_ref, ng_ref, s0_ref, o_ref, sf_ref, st_ref = refs
    else:
        z_ref, q_ref, v_ref, lbp_ref, ng_ref, o_ref, sf_ref, st_ref = refs
    c = pl.program_id(1)
    nc = pl.num_programs(1)
    n = HG_CHUNK

    lbp = lbp_ref[...]
    e = jnp.exp(lbp - jnp.max(lbp, axis=0, keepdims=True))
    soft = e / jnp.sum(e, axis=0, keepdims=True)
    run = soft[0:1]
    for l in range(1, layer + 1):
        run = run + soft[l:l + 1]
    lb_all = run - soft[0:1]

    row = lax.broadcasted_iota(jnp.int32, (n, 1), 0)
    valid = None if valid_len is None else (c * n + row) < valid_len
    sub = jnp.bitwise_and(row, HG_SUB - 1)
    delta, level = _pair_classes(n)
    is_delta = [delta == d for d in range(HG_SUB)]
    halves = []
    half = HG_SUB
    while half < n:
        halves.append(half)
        half *= 2
    is_level = [level == m for m in range(len(halves))]
    is_second = [jnp.bitwise_and(row, half) != 0 for half in halves]

    @pl.when(c == 0)
    def _():
        for h in range(HEADS):
            st_ref[h] = s0_ref[0, h].T if has_state else jnp.zeros((HEAD_DIM, HEAD_DIM), F32)

    for h in range(HEADS):
        hs = _head(h)
        z = z_ref[:, hs] * LOG2E
        qc = q_ref[:, hs]
        v_bf = v_ref[:, hs]
        lb = lb_all[:, hs]
        ez = jnp.exp2(-jnp.abs(z))
        a = jnp.log2(jnp.maximum(lb, LB_FLOOR))
        b = jnp.log1p(-lb) * LOG2E + (jnp.minimum(z, 0.0) - jnp.log2(1.0 + ez))
        log_f = jnp.maximum(a, b) + jnp.log2(1.0 + jnp.exp2(-jnp.abs(a - b)))
        kk = (1.0 - lb) * (jnp.where(z >= 0.0, ez, 1.0) / (1.0 + ez))
        if valid_len is not None:
            log_f = jnp.where(valid, log_f, 0.0)
            kk = jnp.where(valid, kk, 0.0)
        qs = qc * _sigmoid(qc)
        cum = _cumsum_rows(log_f, sub)
        st = st_ref[h]

        att = jnp.where(is_delta[0], jnp.sum(qs * kk, axis=-1, keepdims=True), 0.0)
        for d in range(1, HG_SUB):
            w = jnp.exp2(jnp.minimum(cum - _group_roll(cum, d), 0.0))
            a_d = jnp.sum(qs * _group_roll(kk, d) * w, axis=-1, keepdims=True)
            att = jnp.where(is_delta[d], a_d, att)

        for m, half in enumerate(halves):
            blk = 2 * half
            ref = jnp.concatenate([jnp.broadcast_to(cum[s + half - 1:s + half], (blk, HEAD_DIM))
                                   for s in range(0, n, blk)], axis=0)
            x = (jnp.where(is_second[m], qs, kk) * jnp.exp2(-jnp.abs(cum - ref))).astype(BF16)
            a_m = lax.dot_general(x, x, (((1,), (1,)), ((), ())), preferred_element_type=F32)
            att = jnp.where(is_level[m], a_m, att)

        o = jnp.dot(att.astype(BF16), v_bf, preferred_element_type=F32)
        o = o + lax.dot_general((qs * jnp.exp2(cum)).astype(BF16), st.astype(BF16),
                                (((1,), (1,)), ((), ())), preferred_element_type=F32)
        last = cum[n - 1:n]
        kd = kk * jnp.exp2(last - cum)
        st_new = st * jnp.exp2(last) + lax.dot_general(v_bf, kd.astype(BF16), (((0,), (0,)), ((), ())),
                                                      preferred_element_type=F32)
        st_ref[h] = st_new

        ms = jnp.mean(o * o, axis=-1, keepdims=True)
        o_ref[:, hs] = (o * lax.rsqrt(ms + EPS) * ng_ref[...]).astype(o_ref.dtype)

    @pl.when(c == nc - 1)
    def _():
        for h in range(HEADS):
            sf_ref[0, h] = st_ref[h].T


def _hgrn(proj32, proj16, lb_param, norm_g, state0, layer, *, nb, seq, row0, valid_len):
    n = HG_CHUNK
    nc = seq // n
    rb0 = row0 // n
    has_state = state0 is not None
    depth = lb_param.shape[0]

    def col(cidx):
        return pl.BlockSpec((n, WIDTH), lambda b, c: (rb0 + b * nc + c, cidx))

    in_specs = [col(P32_FC), col(P32_QC), col(P16_IC),
                pl.BlockSpec((depth, WIDTH), lambda b, c: (0, 0)),
                pl.BlockSpec((1, HEAD_DIM), lambda b, c: (0, 0))]
    args = [proj32, proj32, proj16, lb_param, norm_g.reshape(1, HEAD_DIM)]
    if has_state:
        in_specs.append(pl.BlockSpec((1, HEADS, HEAD_DIM, HEAD_DIM), lambda b, c: (b, 0, 0, 0)))
        args.append(state0)
    return pl.pallas_call(
        functools.partial(_hgrn_kernel, layer=layer, valid_len=valid_len, has_state=has_state),
        grid=(nb, nc),
        in_specs=in_specs,
        out_specs=[
            pl.BlockSpec((n, WIDTH), lambda b, c: (b * nc + c, 0)),
            pl.BlockSpec((1, HEADS, HEAD_DIM, HEAD_DIM), lambda b, c: (b, 0, 0, 0)),
        ],
        out_shape=[
            jax.ShapeDtypeStruct((nb * seq, WIDTH), BF16),
            jax.ShapeDtypeStruct((nb, HEADS, HEAD_DIM, HEAD_DIM), F32),
        ],
        scratch_shapes=[pltpu.VMEM((HEADS, HEAD_DIM, HEAD_DIM), F32)],
        compiler_params=_params("parallel", "arbitrary"),
        name="hgrn",
    )(*args)


def _merge_kernel(h_ref, yap_ref, ybp_ref, ycp_ref, yas_ref, ybs_ref, ycs_ref, ga_ref, gb_ref, gc_ref,
                  ma_ref, mb_ref, mc_ref, wb_ref, wo_ref, o_ref, *, prompt_tiles):
    is_prompt = pl.program_id(0) < prompt_tiles
    merged = None
    for n, (yp_ref, ys_ref, g_ref, m_ref) in enumerate(((yap_ref, yas_ref, ga_ref, ma_ref),
                                                        (ybp_ref, ybs_ref, gb_ref, mb_ref),
                                                        (ycp_ref, ycs_ref, gc_ref, mc_ref))):
        g = g_ref[...].astype(F32)
        y = jnp.where(is_prompt, yp_ref[...], ys_ref[...]).astype(F32)
        y = (y * (g * _sigmoid(g))).astype(BF16)
        p = _sigmoid(m_ref[...].astype(F32)) * jnp.dot(y, wb_ref[n], preferred_element_type=F32)
        merged = p if merged is None else merged + p
    o_ref[...] = h_ref[...] + jnp.dot(merged.astype(BF16), wo_ref[...], preferred_element_type=F32)


def _merge(h_all, y_prompt, y_sample, proj16, wb_bf, wo_bf):
    t = h_all.shape[0]
    tp, ts = y_prompt[0].shape[0], y_sample[0].shape[0]
    tm = next(c for c in (256, 128) if tp % c == 0 and ts % c == 0)
    assert tp + ts == t
    n_p = tp // tm
    row = pl.BlockSpec((tm, WIDTH), lambda i: (i, 0))
    prow = pl.BlockSpec((tm, WIDTH), lambda i: (jnp.minimum(i, n_p - 1), 0))
    srow = pl.BlockSpec((tm, WIDTH), lambda i: (jnp.maximum(i - n_p, 0), 0))

    def col(cidx):
        return pl.BlockSpec((tm, WIDTH), lambda i: (i, cidx))

    return pl.pallas_call(
        functools.partial(_merge_kernel, prompt_tiles=n_p),
        grid=(t // tm,),
        in_specs=[row, prow, prow, prow, srow, srow, srow,
                  col(P16_GA), col(P16_GB), col(P16_GC), col(P16_MA), col(P16_MB), col(P16_MC),
                  pl.BlockSpec((3, WIDTH, D_MODEL), lambda i: (0, 0, 0)),
                  pl.BlockSpec((D_MODEL, D_MODEL), lambda i: (0, 0))],
        out_specs=row,
        out_shape=jax.ShapeDtypeStruct((t, D_MODEL), F32),
        compiler_params=_params("arbitrary"),
        name="merge",
    )(h_all, *y_prompt, *y_sample, *([proj16] * 6), wb_bf, wo_bf)


def kernel(x_prompt, x_sample, cache_k, cache_v, state_pool, state_hgrn, meta_tokens, norm_g, w_in,
           q_norm_g, k_norm_g, w_pool, pool_scale, hgrn_lower_bounds, hgrn_norm_g, w_branch, w_out):
    bp, seq, _ = x_prompt.shape
    bs, dseq, _ = x_sample.shape
    depth = w_in.shape[0]
    past = cache_k.shape[2]
    lp_real = N_META + seq
    lp = -(-lp_real // LANE) * LANE
    tp = bp * lp
    ts = bs * dseq

    meta = jnp.broadcast_to(meta_tokens.astype(F32)[None], (bp, N_META, D_MODEL))
    hp = jnp.concatenate([meta, x_prompt, jnp.zeros((bp, lp - lp_real, D_MODEL), F32)], axis=1)
    h_all = jnp.concatenate([hp.reshape(tp, D_MODEL), x_sample.reshape(ts, D_MODEL)], axis=0)

    cache_k2 = cache_k.reshape(depth, bs, past * HEADS, HEAD_DIM)
    cache_v2 = cache_v.reshape(depth, bs, past * HEADS, HEAD_DIM)
    prefix16 = jnp.pad(state_pool, ((0, 0), (0, 0), (1, 0), (0, 0)))

    kp, vp, pp, sp, ksm, vsm, psm, ssm = [], [], [], [], [], [], [], []
    for l in range(depth):
        proj32, proj16 = _in_proj(h_all, norm_g[l], w_in[l].astype(BF16), q_norm_g[l], k_norm_g[l])
        w_pool_bf = w_pool[l].astype(BF16)

        def cols(cidx, r0, r1):
            return proj32[r0:r1, cidx * WIDTH:(cidx + 1) * WIDTH]

        ya_p = _pool(proj32, None, w_pool_bf, pool_scale[l], nb=bp, seq=lp, row0=0, pos0=0)
        yb_p = _attn_prompt(proj16, nb=bp, seq=lp)
        yc_p, st_p = _hgrn(proj32, proj16, hgrn_lower_bounds, hgrn_norm_g[l], None, l,
                           nb=bp, seq=lp, row0=0, valid_len=lp_real)
        ya_s = _pool(proj32, prefix16[l], w_pool_bf, pool_scale[l], nb=bs, seq=dseq, row0=tp, pos0=past)
        yb_s = _attn_sample(proj16, cache_k2, cache_v2, l, nb=bs, seq=dseq, row0=tp)
        yc_s, st_s = _hgrn(proj32, proj16, hgrn_lower_bounds, hgrn_norm_g[l], state_hgrn[l], l,
                           nb=bs, seq=dseq, row0=tp, valid_len=None)

        h_all = _merge(h_all, (ya_p, yb_p, yc_p), (ya_s, yb_s, yc_s), proj16,
                       w_branch[l].astype(BF16), w_out[l].astype(BF16))

        def prompt_rows(cidx, r0, r1):
            return cols(cidx, 0, tp).reshape(bp, lp, WIDTH)[:, r0:r1]

        kp.append(prompt_rows(P32_KB, 0, lp_real).reshape(bp, lp_real, HEADS, HEAD_DIM))
        vp.append(prompt_rows(P32_VB, 0, lp_real).reshape(bp, lp_real, HEADS, HEAD_DIM))
        pp.append(prompt_rows(P32_UA, lp_real - (POOL_MAX - 1), lp_real))
        sp.append(st_p)
        ksm.append(cols(P32_KB, tp, tp + ts).reshape(bs, dseq, HEADS, HEAD_DIM))
        vsm.append(cols(P32_VB, tp, tp + ts).reshape(bs, dseq, HEADS, HEAD_DIM))
        full_s = jnp.concatenate([state_pool[l], cols(P32_UA, tp, tp + ts).reshape(bs, dseq, WIDTH)], axis=1)
        psm.append(full_s[:, -(POOL_MAX - 1):])
        ssm.append(st_s)

    y_prompt = h_all[:tp].reshape(bp, lp, D_MODEL)[:, N_META:lp_real]
    y_sample = h_all[tp:].reshape(bs, dseq, D_MODEL)
    return (y_prompt, y_sample, jnp.stack(kp), jnp.stack(vp), jnp.stack(pp), jnp.stack(sp),
            jnp.stack(ksm), jnp.stack(vsm), jnp.stack(psm), jnp.stack(ssm))
```

```python
import functools

import jax
import jax.numpy as jnp
from jax import lax
from jax.experimental import pallas as pl
from jax.experimental.pallas import tpu as pltpu

F32 = jnp.float32
BF16 = jnp.bfloat16

D_MODEL = 1024
WIDTH = 1024
N_SPLIT = 13
N_META = 16
POOL_WINDOWS = (2, 4, 8, 16)
POOL_GW = WIDTH // len(POOL_WINDOWS)
POOL_MAX = max(POOL_WINDOWS)
HEADS = 8
HEAD_DIM = WIDTH // HEADS
SB_SCALE = HEAD_DIM ** -0.5
LOG2E = 1.4426950408889634
EPS = 1e-6
LB_FLOOR = 1e-30
HG_CHUNK = 64
SUBLANES = 8
HG_SUB = 4
LANE = 128
TK_WIDE = 256
VMEM_LIMIT = 56 * 1024 * 1024

C_UA, C_GA, C_QB, C_KB, C_VB, C_GB, C_FC, C_QC, C_IC, C_GC, C_MA, C_MB, C_MC = range(N_SPLIT)


def _head(h):
    return slice(h * HEAD_DIM, (h + 1) * HEAD_DIM)


def _sigmoid(x):
    return 1.0 / (1.0 + jnp.exp(-x))


def _log1pexp_neg_abs(x):
    return jnp.log(1.0 + jnp.exp(-jnp.abs(x)))


def _softplus(x):
    return jnp.maximum(x, 0.0) + _log1pexp_neg_abs(x)


def _params(*sem):
    return pltpu.CompilerParams(dimension_semantics=sem, vmem_limit_bytes=VMEM_LIMIT)


STEP_GROUPS = (C_UA, C_KB, C_VB, C_FC, C_QC, C_QB, C_IC, C_GA, C_GB, C_GC, C_MA, C_MB, C_MC)
P32_UA, P32_FC, P32_QC = range(3)
N_P32 = 3
P16_KB, P16_VB, P16_QB, P16_IC, P16_GA, P16_GB, P16_GC, P16_MA, P16_MB, P16_MC = range(10)
N_P16 = 10
STEP_P32 = (0, 0, 0, 1, 2, 2, 2, 2, 2, 2, 2, 2, 2)
STEP_P16 = (0, 0, 1, 1, 1, 2, 3, 4, 5, 6, 7, 8, 9)
S_UA, S_KB, S_VB, S_FC, S_QC, S_QB, S_IC, S_GA = range(8)
S_MA = S_GA + 3


def _in_proj_kernel(tab_ref, x_ref, g_ref, w_ref, qg_ref, kg_ref, o32_ref, o16_ref, kh_ref, vh_ref, xn_ref):
    j = pl.program_id(1)
    tm = x_ref.shape[0]

    @pl.when(j == 0)
    def _():
        x = x_ref[...]
        ms = jnp.mean(x * x, axis=-1, keepdims=True)
        xn_ref[...] = (x * lax.rsqrt(ms + EPS) * g_ref[...]).astype(BF16)

    def project():
        return jnp.dot(xn_ref[...], w_ref[...], preferred_element_type=F32)

    def head_norm(a, gain):
        ms = jnp.mean(a * a, axis=-1, keepdims=True)
        return a * lax.rsqrt(ms + EPS) * gain

    def head_rows(h):
        return pl.ds(h, tm, stride=HEADS)

    @pl.when(jnp.logical_or(j == S_UA, jnp.logical_or(j == S_FC, j == S_QC)))
    def _():
        o32_ref[...] = project()

    @pl.when(j == S_KB)
    def _():
        acc = project()
        for h in range(HEADS):
            kn = head_norm(acc[:, _head(h)], kg_ref[...])
            kh_ref[head_rows(h), :] = kn
            o16_ref[:, _head(h)] = kn.astype(BF16)

    @pl.when(j == S_VB)
    def _():
        acc = project()
        for h in range(HEADS):
            vh_ref[head_rows(h), :] = acc[:, _head(h)]
        o16_ref[...] = acc.astype(BF16)

    @pl.when(j == S_QB)
    def _():
        acc = project()
        for h in range(HEADS):
            qn = head_norm(acc[:, _head(h)], qg_ref[...])
            o16_ref[:, _head(h)] = (qn * (SB_SCALE * LOG2E)).astype(BF16)

    @pl.when(j == S_IC)
    def _():
        o16_ref[...] = project().astype(BF16)

    @pl.when(jnp.logical_and(j >= S_GA, j < S_MA))
    def _():
        g = project()
        o16_ref[...] = (g * _sigmoid(g)).astype(BF16)

    @pl.when(j >= S_MA)
    def _():
        o16_ref[...] = _sigmoid(project()).astype(BF16)


def _pick_tile(t, candidates):
    return next(c for c in candidates if t % c == 0)


def _in_proj(h_all, norm_g, w_bf, q_g, k_g):
    t = h_all.shape[0]
    tm = _pick_tile(t, (1024, 512, 256, 128))
    table = jnp.array([STEP_GROUPS, STEP_P32, STEP_P16], jnp.int32)
    grid_spec = pltpu.PrefetchScalarGridSpec(
        num_scalar_prefetch=1,
        grid=(t // tm, N_SPLIT),
        in_specs=[
            pl.BlockSpec((tm, D_MODEL), lambda i, j, tab: (i, 0)),
            pl.BlockSpec((1, D_MODEL), lambda i, j, tab: (0, 0)),
            pl.BlockSpec((D_MODEL, WIDTH), lambda i, j, tab: (0, tab[0, j])),
            pl.BlockSpec((1, HEAD_DIM), lambda i, j, tab: (0, 0)),
            pl.BlockSpec((1, HEAD_DIM), lambda i, j, tab: (0, 0)),
        ],
        out_specs=[
            pl.BlockSpec((tm, WIDTH), lambda i, j, tab: (i, tab[1, j])),
            pl.BlockSpec((tm, WIDTH), lambda i, j, tab: (i, tab[2, j])),
            pl.BlockSpec((tm * HEADS, HEAD_DIM), lambda i, j, tab: (i, 0)),
            pl.BlockSpec((tm * HEADS, HEAD_DIM), lambda i, j, tab: (i, 0)),
        ],
        scratch_shapes=[pltpu.VMEM((tm, D_MODEL), BF16)],
    )
    return pl.pallas_call(
        _in_proj_kernel,
        grid_spec=grid_spec,
        out_shape=[jax.ShapeDtypeStruct((t, N_P32 * WIDTH), F32),
                   jax.ShapeDtypeStruct((t, N_P16 * WIDTH), BF16),
                   jax.ShapeDtypeStruct((t * HEADS, HEAD_DIM), F32),
                   jax.ShapeDtypeStruct((t * HEADS, HEAD_DIM), F32)],
        compiler_params=_params("arbitrary", "arbitrary"),
        name="in_proj",
    )(table, h_all, norm_g.reshape(1, D_MODEL), w_bf, q_g.reshape(1, HEAD_DIM), k_g.reshape(1, HEAD_DIM))


def _pool_kernel(*refs, seq, pos0, has_prefix):
    if has_prefix:
        u_ref, p_ref, w_ref, s_ref, o_ref = refs
    else:
        u_ref, w_ref, s_ref, o_ref = refs
    g = pl.program_id(1)
    u = u_ref[...]
    head = p_ref[0] if has_prefix else jnp.zeros((POOL_MAX, POOL_GW), F32)
    x = jnp.concatenate([head, u], axis=0)
    pos = pos0 + lax.broadcasted_iota(jnp.int32, (seq, 1), 0)

    for gi, win in enumerate(POOL_WINDOWS):
        @pl.when(g == gi)
        def _(win=win):
            s = x
            sh = 1
            while sh < win:
                s = s + pltpu.roll(s, sh, 0)
                sh *= 2
            cnt = jnp.minimum(pos + 1, win).astype(F32)
            diff = s[POOL_MAX:] / cnt - u
            y = jnp.dot(diff.astype(BF16), w_ref[0], preferred_element_type=F32)
            o_ref[...] = (y * s_ref[...]).astype(o_ref.dtype)


def _pool(proj32, prefix16, w_pool_bf, scale, *, nb, seq, row0, pos0):
    rb0 = row0 // seq
    has_prefix = prefix16 is not None
    ua0 = P32_UA * len(POOL_WINDOWS)
    in_specs = [pl.BlockSpec((seq, POOL_GW), lambda b, g: (rb0 + b, ua0 + g))]
    args = [proj32]
    if has_prefix:
        in_specs.append(pl.BlockSpec((1, POOL_MAX, POOL_GW), lambda b, g: (b, 0, g)))
        args.append(prefix16)
    in_specs += [
        pl.BlockSpec((1, POOL_GW, POOL_GW), lambda b, g: (g, 0, 0)),
        pl.BlockSpec((1, POOL_GW), lambda b, g: (0, g)),
    ]
    args += [w_pool_bf, scale.reshape(1, WIDTH)]
    return pl.pallas_call(
        functools.partial(_pool_kernel, seq=seq, pos0=pos0, has_prefix=has_prefix),
        grid=(nb, len(POOL_WINDOWS)),
        in_specs=in_specs,
        out_specs=pl.BlockSpec((seq, POOL_GW), lambda b, g: (b, g)),
        out_shape=jax.ShapeDtypeStruct((nb * seq, WIDTH), BF16),
        compiler_params=_params("parallel", "parallel"),
        name="pool",
    )(*args)


def _tri_after(n):
    r = lax.broadcasted_iota(jnp.int32, (n, n), 0)
    c = lax.broadcasted_iota(jnp.int32, (n, n), 1)
    return jnp.where(r > c, 1.0, 0.0).astype(BF16)


def _sb_heads(qs_ref, key, val, tk, tri, carry_ref, keep_ref, logit_ref, o_ref, first):
    tq = qs_ref.shape[0]
    if first:
        r = lax.broadcasted_iota(jnp.int32, (tq, tk), 0)
        c = lax.broadcasted_iota(jnp.int32, (tq, tk), 1)
        mask = c < r
    for h in range(HEADS):
        rows = slice(h * tq, (h + 1) * tq)
        z = lax.dot_general(qs_ref[:, _head(h)], key(h), (((1,), (1,)), ((), ())),
                            preferred_element_type=F32)
        soft = jnp.maximum(z, 0.0) + jnp.log2(1.0 + jnp.exp2(-jnp.abs(z)))
        keep = -soft
        if first:
            keep = jnp.where(mask, keep, 0.0)
            logit_ref[rows, :tk] = z - soft
            carry_ref[h] = jnp.sum(keep, axis=1, keepdims=True)
        else:
            carry = carry_ref[h]
            logit_ref[rows, :tk] = z - soft + carry
            carry_ref[h] = carry + jnp.sum(keep, axis=1, keepdims=True)
        keep_ref[rows, :tk] = keep.astype(BF16)
    later = jnp.dot(keep_ref[:, :tk], tri, preferred_element_type=F32)
    for h in range(HEADS):
        rows = slice(h * tq, (h + 1) * tq)
        wts = jnp.exp2(logit_ref[rows, :tk] + later[rows])
        if first:
            wts = jnp.where(mask, wts, 0.0)
        pv = jnp.dot(wts.astype(BF16), val(h), preferred_element_type=F32)
        if first:
            o_ref[:, _head(h)] = pv
        else:
            o_ref[:, _head(h)] += pv


def _attn_scratch(tq):
    return [pltpu.VMEM((tq, WIDTH), F32),
            pltpu.VMEM((TK_WIDE, TK_WIDE), BF16),
            pltpu.VMEM((HEADS, tq, 1), F32),
            pltpu.VMEM((HEADS * tq, TK_WIDE), BF16),
            pltpu.VMEM((HEADS * tq, TK_WIDE), F32)]


def _attn_prompt_kernel(q_ref, k_ref, v_ref, o_ref, acc_ref, triw_ref, carry_ref, keep_ref, logit_ref):
    i = pl.program_id(1)
    tq = q_ref.shape[0]
    triw_ref[...] = _tri_after(TK_WIDE)
    d0 = pl.multiple_of(i * tq, tq)

    def sweep(k0, tk, tri, first):
        _sb_heads(q_ref, lambda h: k_ref[pl.ds(k0, tk), _head(h)], lambda h: v_ref[pl.ds(k0, tk), _head(h)],
                  tk, tri, carry_ref, keep_ref, logit_ref, acc_ref, first)

    sweep(d0, tq, _tri_after(tq), True)

    @pl.loop(0, lax.shift_right_logical(i, 1))
    def _(t):
        sweep(pl.multiple_of(d0 - (t + 1) * TK_WIDE, tq), TK_WIDE, triw_ref[...], False)

    @pl.when(jnp.bitwise_and(i, 1) == 1)
    def _():
        sweep(0, tq, _tri_after(tq), False)

    o_ref[...] = acc_ref[...].astype(o_ref.dtype)


def _attn_prompt(proj16, *, nb, seq):
    nq = seq // LANE
    return pl.pallas_call(
        _attn_prompt_kernel,
        grid=(nb, nq),
        in_specs=[
            pl.BlockSpec((LANE, WIDTH), lambda b, i: (b * nq + i, P16_QB)),
            pl.BlockSpec((seq, WIDTH), lambda b, i: (b, P16_KB)),
            pl.BlockSpec((seq, WIDTH), lambda b, i: (b, P16_VB)),
        ],
        out_specs=pl.BlockSpec((LANE, WIDTH), lambda b, i: (b * nq + i, 0)),
        out_shape=jax.ShapeDtypeStruct((nb * seq, WIDTH), BF16),
        scratch_shapes=_attn_scratch(LANE),
        compiler_params=_params("parallel", "arbitrary"),
        name="attn_prompt",
    )(proj16, proj16, proj16)


def _attn_sample_kernel(q_ref, kn_ref, vn_ref, kp_ref, vp_ref, o_ref, acc_ref, triw_ref, carry_ref, keep_ref,
                        logit_ref, *, past):
    tq = q_ref.shape[0]
    triw_ref[...] = _tri_after(TK_WIDE)
    n_past = past // TK_WIDE

    _sb_heads(q_ref, lambda h: kn_ref[:, _head(h)], lambda h: vn_ref[:, _head(h)],
              tq, _tri_after(tq), carry_ref, keep_ref, logit_ref, acc_ref, True)

    @pl.loop(0, n_past)
    def _(t):
        k0 = (n_past - 1 - t) * TK_WIDE

        def head_rows(ref, h):
            rows = pl.ds(pl.multiple_of(k0 * HEADS, HEADS) + h, TK_WIDE, stride=HEADS)
            return ref[0, 0, rows, :].astype(BF16)

        _sb_heads(q_ref, lambda h: head_rows(kp_ref, h), lambda h: head_rows(vp_ref, h),
                  TK_WIDE, triw_ref[...], carry_ref, keep_ref, logit_ref, acc_ref, False)

    o_ref[...] = acc_ref[...].astype(o_ref.dtype)


def _attn_sample(proj16, cache_k, cache_v, layer, *, nb, seq, row0):
    rows = cache_k.shape[2]
    rb0 = row0 // seq
    return pl.pallas_call(
        functools.partial(_attn_sample_kernel, past=rows // HEADS),
        grid=(nb,),
        in_specs=[
            pl.BlockSpec((seq, WIDTH), lambda b: (rb0 + b, P16_QB)),
            pl.BlockSpec((seq, WIDTH), lambda b: (rb0 + b, P16_KB)),
            pl.BlockSpec((seq, WIDTH), lambda b: (rb0 + b, P16_VB)),
            pl.BlockSpec((1, 1, rows, HEAD_DIM), lambda b: (layer, b, 0, 0)),
            pl.BlockSpec((1, 1, rows, HEAD_DIM), lambda b: (layer, b, 0, 0)),
        ],
        out_specs=pl.BlockSpec((seq, WIDTH), lambda b: (b, 0)),
        out_shape=jax.ShapeDtypeStruct((nb * seq, WIDTH), BF16),
        scratch_shapes=_attn_scratch(seq),
        compiler_params=_params("parallel"),
        name="attn_sample",
    )(proj16, proj16, proj16, cache_k, cache_v)


def _group_roll(x, d):
    n, lanes = x.shape
    return pltpu.roll(x.reshape(n // SUBLANES, SUBLANES, lanes), d, 1).reshape(n, lanes)


def _cumsum_rows(x, sub):
    n = x.shape[0]
    sh = 1
    while sh < SUBLANES:
        x = x + jnp.where(sub >= sh, _group_roll(x, sh), 0.0)
        sh *= 2
    parts, run = [], None
    for g in range(n // SUBLANES):
        blk = x[g * SUBLANES:(g + 1) * SUBLANES]
        if run is not None:
            blk = blk + run
        parts.append(blk)
        run = blk[SUBLANES - 1:SUBLANES]
    return jnp.concatenate(parts, axis=0)


def _half_end_rows(x, half):
    n, lanes = x.shape
    blk = 2 * half
    if blk <= SUBLANES:
        x3 = x.reshape(n // SUBLANES, SUBLANES, lanes)
        parts = [jnp.broadcast_to(x3[:, s + half - 1:s + half, :], (n // SUBLANES, blk, lanes))
                 for s in range(0, SUBLANES, blk)]
        return jnp.concatenate(parts, axis=1).reshape(n, lanes)
    return jnp.concatenate([jnp.broadcast_to(x[s + half - 1:s + half], (blk, lanes))
                            for s in range(0, n, blk)], axis=0)


def _pair_classes(n):
    r = lax.broadcasted_iota(jnp.int32, (n, n), 0)
    c = lax.broadcasted_iota(jnp.int32, (n, n), 1)

    def block_of(x, size):
        return jnp.bitwise_and(x, -size)

    same = block_of(r, HG_SUB) == block_of(c, HG_SUB)
    delta = jnp.where(jnp.logical_and(same, c <= r), r - c, -1)
    level = jnp.full((n, n), -1, jnp.int32)
    half, m = HG_SUB, 0
    while half < n:
        blk = 2 * half
        hit = jnp.logical_and(block_of(r, blk) == block_of(c, blk),
                              jnp.logical_and(jnp.bitwise_and(r, half) != 0, jnp.bitwise_and(c, half) == 0))
        level = jnp.where(hit, m, level)
        half, m = blk, m + 1
    return delta, level


def _hgrn_kernel(*refs, layer, valid_len, has_state):
    refs = list(refs)
    z_ref, q_ref, v_ref, lbp_ref, ng_ref = refs[:5]
    s0_ref = refs[5] if has_state else None
    o_ref, sf_ref, st_ref, pair_ref, lvl_ref, att_ref, qdec_ref, kdec_ref, dec_ref = refs[-9:]
    c = pl.program_id(1)
    nc = pl.num_programs(1)
    n = HG_CHUNK

    lbp = lbp_ref[...]
    e = jnp.exp(lbp - jnp.max(lbp, axis=0, keepdims=True))
    soft = e / jnp.sum(e, axis=0, keepdims=True)
    run = soft[0:1]
    for l in range(1, layer + 1):
        run = run + soft[l:l + 1]
    lb_all = run - soft[0:1]

    row = lax.broadcasted_iota(jnp.int32, (n, 1), 0)
    valid = None if valid_len is None else (c * n + row) < valid_len
    sub = jnp.bitwise_and(row, SUBLANES - 1)
    delta, level = _pair_classes(n)
    is_delta = [delta == d for d in range(HG_SUB)]
    halves = []
    half = HG_SUB
    while half < n:
        halves.append(half)
        half *= 2
    is_level = [level == m for m in range(len(halves))]
    is_second = [jnp.bitwise_and(row, half) != 0 for half in halves]

    @pl.when(c == 0)
    def _():
        for h in range(HEADS):
            st_ref[h] = s0_ref[0, 0, h].T if has_state else jnp.zeros((HEAD_DIM, HEAD_DIM), F32)

    for h in range(HEADS):
        hs = _head(h)
        rows = slice(h * n, (h + 1) * n)
        z = z_ref[:, hs] * LOG2E
        qc = q_ref[:, hs]
        lb = lb_all[:, hs]
        ez = jnp.exp2(-jnp.abs(z))
        a = jnp.log2(jnp.maximum(lb, LB_FLOOR))
        b = jnp.log1p(-lb) * LOG2E + (jnp.minimum(z, 0.0) - jnp.log2(1.0 + ez))
        log_f = jnp.maximum(a, b) + jnp.log2(1.0 + jnp.exp2(-jnp.abs(a - b)))
        kk = (1.0 - lb) * (jnp.where(z >= 0.0, ez, 1.0) / (1.0 + ez))
        if valid_len is not None:
            log_f = jnp.where(valid, log_f, 0.0)
            kk = jnp.where(valid, kk, 0.0)
        qs = qc * _sigmoid(qc)
        cum = _cumsum_rows(log_f, sub)

        att = jnp.where(is_delta[0], jnp.sum(qs * kk, axis=-1, keepdims=True), 0.0)
        for d in range(1, HG_SUB):
            w = jnp.exp2(jnp.minimum(cum - _group_roll(cum, d), 0.0))
            a_d = jnp.sum(qs * _group_roll(kk, d) * w, axis=-1, keepdims=True)
            att = jnp.where(is_delta[d], a_d, att)
        pair_ref[h] = att

        for m, half in enumerate(halves):
            x = jnp.where(is_second[m], qs, kk) * jnp.exp2(-jnp.abs(cum - _half_end_rows(cum, half)))
            lvl_ref[m, rows] = x.astype(BF16)

        last = cum[n - 1:n]
        qdec_ref[rows] = (qs * jnp.exp2(cum)).astype(BF16)
        kdec_ref[rows] = (kk * jnp.exp2(last - cum)).astype(BF16)
        dec_ref[h] = jnp.exp2(last)

    for h in range(HEADS):
        rows = slice(h * n, (h + 1) * n)
        att = pair_ref[h]
        for m in range(len(halves)):
            x = lvl_ref[m, rows]
            a_m = lax.dot_general(x, x, (((1,), (1,)), ((), ())), preferred_element_type=F32)
            att = jnp.where(is_level[m], a_m, att)
        att_ref[h] = att.astype(BF16)

    for h in range(HEADS):
        hs = _head(h)
        rows = slice(h * n, (h + 1) * n)
        v_bf = v_ref[:, hs]
        st = st_ref[h]
        o = jnp.dot(att_ref[h], v_bf, preferred_element_type=F32)
        o = o + lax.dot_general(qdec_ref[rows], st.astype(BF16), (((1,), (1,)), ((), ())),
                                preferred_element_type=F32)
        st_ref[h] = st * dec_ref[h] + lax.dot_general(v_bf, kdec_ref[rows], (((0,), (0,)), ((), ())),
                                                      preferred_element_type=F32)
        ms = jnp.mean(o * o, axis=-1, keepdims=True)
        o_ref[:, hs] = (o * lax.rsqrt(ms + EPS) * ng_ref[...]).astype(o_ref.dtype)

    @pl.when(c == nc - 1)
    def _():
        for h in range(HEADS):
            sf_ref[0, h] = st_ref[h].T


def _hgrn(proj32, proj16, lb_param, norm_g, state0, layer, *, nb, seq, row0, valid_len):
    n = HG_CHUNK
    nc = seq // n
    rb0 = row0 // n
    has_state = state0 is not None
    depth = lb_param.shape[0]
    n_levels = (n // HG_SUB).bit_length() - 1

    def col(cidx):
        return pl.BlockSpec((n, WIDTH), lambda b, c: (rb0 + b * nc + c, cidx))

    in_specs = [col(P32_FC), col(P32_QC), col(P16_IC),
                pl.BlockSpec((depth, WIDTH), lambda b, c: (0, 0)),
                pl.BlockSpec((1, HEAD_DIM), lambda b, c: (0, 0))]
    args = [proj32, proj32, proj16, lb_param, norm_g.reshape(1, HEAD_DIM)]
    if has_state:
        in_specs.append(pl.BlockSpec((1, 1, HEADS, HEAD_DIM, HEAD_DIM), lambda b, c: (layer, b, 0, 0, 0)))
        args.append(state0)
    return pl.pallas_call(
        functools.partial(_hgrn_kernel, layer=layer, valid_len=valid_len, has_state=has_state),
        grid=(nb, nc),
        in_specs=in_specs,
        out_specs=[
            pl.BlockSpec((n, WIDTH), lambda b, c: (b * nc + c, 0)),
            pl.BlockSpec((1, HEADS, HEAD_DIM, HEAD_DIM), lambda b, c: (b, 0, 0, 0)),
        ],
        out_shape=[
            jax.ShapeDtypeStruct((nb * seq, WIDTH), BF16),
            jax.ShapeDtypeStruct((nb, HEADS, HEAD_DIM, HEAD_DIM), F32),
        ],
        scratch_shapes=[
            pltpu.VMEM((HEADS, HEAD_DIM, HEAD_DIM), F32),
            pltpu.VMEM((HEADS, n, n), F32),
            pltpu.VMEM((n_levels, HEADS * n, HEAD_DIM), BF16),
            pltpu.VMEM((HEADS, n, n), BF16),
            pltpu.VMEM((HEADS * n, HEAD_DIM), BF16),
            pltpu.VMEM((HEADS * n, HEAD_DIM), BF16),
            pltpu.VMEM((HEADS, 1, HEAD_DIM), F32),
        ],
        compiler_params=_params("parallel", "arbitrary"),
        name="hgrn",
    )(*args)


def _merge_kernel(h_ref, yap_ref, ybp_ref, ycp_ref, yas_ref, ybs_ref, ycs_ref, ga_ref, gb_ref, gc_ref,
                  ma_ref, mb_ref, mc_ref, wb_ref, wo_ref, *o_refs, prompt_tiles):
    is_prompt = pl.program_id(0) < prompt_tiles
    merged = None
    for n, (yp_ref, ys_ref, g_ref, m_ref) in enumerate(((yap_ref, yas_ref, ga_ref, ma_ref),
                                                        (ybp_ref, ybs_ref, gb_ref, mb_ref),
                                                        (ycp_ref, ycs_ref, gc_ref, mc_ref))):
        y = jnp.where(is_prompt, yp_ref[...], ys_ref[...]).astype(F32)
        y = (y * g_ref[...].astype(F32)).astype(BF16)
        p = m_ref[...].astype(F32) * jnp.dot(y, wb_ref[n], preferred_element_type=F32)
        merged = p if merged is None else merged + p
    out = h_ref[...] + jnp.dot(merged.astype(BF16), wo_ref[...], preferred_element_type=F32)
    if len(o_refs) == 1:
        o_refs[0][...] = out
    else:
        @pl.when(is_prompt)
        def _():
            o_refs[0][...] = out

        @pl.when(jnp.logical_not(is_prompt))
        def _():
            o_refs[1][...] = out


def _merge(h_all, y_prompt, y_sample, proj16, wb_bf, wo_bf, split):
    t = h_all.shape[0]
    tp, ts = y_prompt[0].shape[0], y_sample[0].shape[0]
    tm = next(c for c in (256, 128) if tp % c == 0 and ts % c == 0)
    assert tp + ts == t
    n_p = tp // tm
    row = pl.BlockSpec((tm, WIDTH), lambda i: (i, 0))
    prow = pl.BlockSpec((tm, WIDTH), lambda i: (jnp.minimum(i, n_p - 1), 0))
    srow = pl.BlockSpec((tm, WIDTH), lambda i: (jnp.maximum(i - n_p, 0), 0))

    def col(cidx):
        return pl.BlockSpec((tm, WIDTH), lambda i: (i, cidx))

    if split:
        out_specs = [prow, srow]
        out_shape = [jax.ShapeDtypeStruct((tp, D_MODEL), F32), jax.ShapeDtypeStruct((ts, D_MODEL), F32)]
    else:
        out_specs = row
        out_shape = jax.ShapeDtypeStruct((t, D_MODEL), F32)
    return pl.pallas_call(
        functools.partial(_merge_kernel, prompt_tiles=n_p),
        grid=(t // tm,),
        in_specs=[row, prow, prow, prow, srow, srow, srow,
                  col(P16_GA), col(P16_GB), col(P16_GC), col(P16_MA), col(P16_MB), col(P16_MC),
                  pl.BlockSpec((3, WIDTH, D_MODEL), lambda i: (0, 0, 0)),
                  pl.BlockSpec((D_MODEL, D_MODEL), lambda i: (0, 0))],
        out_specs=out_specs,
        out_shape=out_shape,
        compiler_params=_params("arbitrary"),
        name="merge",
    )(h_all, *y_prompt, *y_sample, *([proj16] * 6), wb_bf, wo_bf)


def kernel(x_prompt, x_sample, cache_k, cache_v, state_pool, state_hgrn, meta_tokens, norm_g, w_in,
           q_norm_g, k_norm_g, w_pool, pool_scale, hgrn_lower_bounds, hgrn_norm_g, w_branch, w_out):
    bp, seq, _ = x_prompt.shape
    bs, dseq, _ = x_sample.shape
    depth = w_in.shape[0]
    past = cache_k.shape[2]
    lp_real = N_META + seq
    lp = -(-lp_real // LANE) * LANE
    tp = bp * lp
    ts = bs * dseq

    meta = jnp.broadcast_to(meta_tokens.astype(F32)[None], (bp, N_META, D_MODEL))
    hp = jnp.concatenate([meta, x_prompt, jnp.zeros((bp, lp - lp_real, D_MODEL), F32)], axis=1)
    h_all = jnp.concatenate([hp.reshape(tp, D_MODEL), x_sample.reshape(ts, D_MODEL)], axis=0)

    cache_k2 = cache_k.reshape(depth, bs, past * HEADS, HEAD_DIM)
    cache_v2 = cache_v.reshape(depth, bs, past * HEADS, HEAD_DIM)
    prefix16 = jnp.pad(state_pool, ((0, 0), (0, 0), (1, 0), (0, 0)))

    kp, vp, pp, sp, ksm, vsm, psm, ssm = [], [], [], [], [], [], [], []
    for l in range(depth):
        proj32, proj16, keys, vals = _in_proj(h_all, norm_g[l], w_in[l].astype(BF16), q_norm_g[l], k_norm_g[l])
        w_pool_bf = w_pool[l].astype(BF16)

        ya_p = _pool(proj32, None, w_pool_bf, pool_scale[l], nb=bp, seq=lp, row0=0, pos0=0)
        yb_p = _attn_prompt(proj16, nb=bp, seq=lp)
        yc_p, st_p = _hgrn(proj32, proj16, hgrn_lower_bounds, hgrn_norm_g[l], None, l,
                           nb=bp, seq=lp, row0=0, valid_len=lp_real)
        ya_s = _pool(proj32, prefix16[l], w_pool_bf, pool_scale[l], nb=bs, seq=dseq, row0=tp, pos0=past)
        yb_s = _attn_sample(proj16, cache_k2, cache_v2, l, nb=bs, seq=dseq, row0=tp)
        yc_s, st_s = _hgrn(proj32, proj16, hgrn_lower_bounds, hgrn_norm_g[l], state_hgrn, l,
                           nb=bs, seq=dseq, row0=tp, valid_len=None)

        h_all = _merge(h_all, (ya_p, yb_p, yc_p), (ya_s, yb_s, yc_s), proj16,
                       w_branch[l].astype(BF16), w_out[l].astype(BF16), split=(l == depth - 1))

        def pool_rows(r0, r1):
            return proj32[r0:r1, P32_UA * WIDTH:(P32_UA + 1) * WIDTH]

        kp.append(keys[:tp * HEADS].reshape(bp, lp, HEADS, HEAD_DIM)[:, :lp_real])
        vp.append(vals[:tp * HEADS].reshape(bp, lp, HEADS, HEAD_DIM)[:, :lp_real])
        pp.append(jnp.stack([pool_rows(b * lp + lp_real - (POOL_MAX - 1), b * lp + lp_real) for b in range(bp)]))
        sp.append(st_p)
        ksm.append(keys[tp * HEADS:].reshape(bs, dseq, HEADS, HEAD_DIM))
        vsm.append(vals[tp * HEADS:].reshape(bs, dseq, HEADS, HEAD_DIM))
        full_s = jnp.concatenate([state_pool[l], pool_rows(tp, tp + ts).reshape(bs, dseq, WIDTH)], axis=1)
        psm.append(full_s[:, -(POOL_MAX - 1):])
        ssm.append(st_s)

    h_prompt, h_sample = h_all
    y_prompt = h_prompt.reshape(bp, lp, D_MODEL)[:, N_META:lp_real]
    y_sample = h_sample.reshape(bs, dseq, D_MODEL)
    return (y_prompt, y_sample, jnp.stack(kp), jnp.stack(vp), jnp.stack(pp), jnp.stack(sp),
            jnp.stack(ksm), jnp.stack(vsm), jnp.stack(psm), jnp.stack(ssm))
```

```python
import functools
import math

import jax
import jax.numpy as jnp
from jax import lax
from jax.experimental import pallas as pl
from jax.experimental.pallas import tpu as pltpu

F32 = jnp.float32
BF16 = jnp.bfloat16

D_MODEL = 1024
WIDTH = 1024
N_SPLIT = 13
N_META = 16
POOL_WINDOWS = (2, 4, 8, 16)
POOL_GW = WIDTH // len(POOL_WINDOWS)
POOL_MAX = max(POOL_WINDOWS)
HEADS = 8
HEAD_DIM = WIDTH // HEADS
SB_SCALE = HEAD_DIM ** -0.5
LOG2E = 1.4426950408889634
EPS = 1e-6
LB_FLOOR = 1e-30
HG_CHUNK = 64
SUBLANES = 8
HG_SUB = 4
LANE = 128
TK_WIDE = 256
VMEM_LIMIT = 56 * 1024 * 1024

C_UA, C_GA, C_QB, C_KB, C_VB, C_GB, C_FC, C_QC, C_IC, C_GC, C_MA, C_MB, C_MC = range(N_SPLIT)


def _head(h):
    return slice(h * HEAD_DIM, (h + 1) * HEAD_DIM)


def _sigmoid(x):
    return 1.0 / (1.0 + jnp.exp(-x))


def _log1pexp_neg_abs(x):
    return jnp.log(1.0 + jnp.exp(-jnp.abs(x)))


def _softplus(x):
    return jnp.maximum(x, 0.0) + _log1pexp_neg_abs(x)


def _params(*sem):
    return pltpu.CompilerParams(dimension_semantics=sem, vmem_limit_bytes=VMEM_LIMIT)


STEP_GROUPS = (C_UA, C_KB, C_VB, C_FC, C_QC, C_QB, C_IC, C_GA, C_GB, C_GC, C_MA, C_MB, C_MC)
P32_UA, P32_FC, P32_QC = range(3)
N_P32 = 3
P16_KB, P16_VB, P16_QB, P16_IC, P16_GA, P16_GB, P16_GC, P16_MA, P16_MB, P16_MC = range(10)
N_P16 = 10
STEP_P32 = (0, 0, 0, 1, 2, 2, 2, 2, 2, 2, 2, 2, 2)
STEP_P16 = (0, 0, 1, 1, 1, 2, 3, 4, 5, 6, 7, 8, 9)
S_UA, S_KB, S_VB, S_FC, S_QC, S_QB, S_IC, S_GA = range(8)
S_MA = S_GA + 3


def _in_proj_kernel(tab_ref, x_ref, g_ref, w_ref, qg_ref, kg_ref, o32_ref, o16_ref, kh_ref, vh_ref, xn_ref):
    j = pl.program_id(1)
    tm = x_ref.shape[0]

    @pl.when(j == 0)
    def _():
        x = x_ref[...]
        ms = jnp.mean(x * x, axis=-1, keepdims=True)
        xn_ref[...] = (x * lax.rsqrt(ms + EPS) * g_ref[...]).astype(BF16)

    def project():
        return jnp.dot(xn_ref[...], w_ref[...], preferred_element_type=F32)

    def head_norm(a, gain):
        ms = jnp.mean(a * a, axis=-1, keepdims=True)
        return a * lax.rsqrt(ms + EPS) * gain

    def head_rows(h):
        return pl.ds(h, tm, stride=HEADS)

    @pl.when(jnp.logical_or(j == S_UA, jnp.logical_or(j == S_FC, j == S_QC)))
    def _():
        o32_ref[...] = project()

    @pl.when(j == S_KB)
    def _():
        acc = project()
        for h in range(HEADS):
            kn = head_norm(acc[:, _head(h)], kg_ref[...])
            kh_ref[head_rows(h), :] = kn
            o16_ref[:, _head(h)] = kn.astype(BF16)

    @pl.when(j == S_VB)
    def _():
        acc = project()
        for h in range(HEADS):
            vh_ref[head_rows(h), :] = acc[:, _head(h)]
        o16_ref[...] = acc.astype(BF16)

    @pl.when(j == S_QB)
    def _():
        acc = project()
        for h in range(HEADS):
            qn = head_norm(acc[:, _head(h)], qg_ref[...])
            o16_ref[:, _head(h)] = (qn * (SB_SCALE * LOG2E)).astype(BF16)

    @pl.when(j == S_IC)
    def _():
        o16_ref[...] = project().astype(BF16)

    @pl.when(jnp.logical_and(j >= S_GA, j < S_MA))
    def _():
        g = project()
        o16_ref[...] = (g * _sigmoid(g)).astype(BF16)

    @pl.when(j >= S_MA)
    def _():
        o16_ref[...] = _sigmoid(project()).astype(BF16)


def _pick_tile(t, candidates):
    return next(c for c in candidates if t % c == 0)


def _in_proj(h_all, norm_g, w_bf, q_g, k_g):
    t = h_all.shape[0]
    tm = _pick_tile(t, (1024, 512, 256, 128))
    table = jnp.array([STEP_GROUPS, STEP_P32, STEP_P16], jnp.int32)
    grid_spec = pltpu.PrefetchScalarGridSpec(
        num_scalar_prefetch=1,
        grid=(t // tm, N_SPLIT),
        in_specs=[
            pl.BlockSpec((tm, D_MODEL), lambda i, j, tab: (i, 0)),
            pl.BlockSpec((1, D_MODEL), lambda i, j, tab: (0, 0)),
            pl.BlockSpec((D_MODEL, WIDTH), lambda i, j, tab: (0, tab[0, j])),
            pl.BlockSpec((1, HEAD_DIM), lambda i, j, tab: (0, 0)),
            pl.BlockSpec((1, HEAD_DIM), lambda i, j, tab: (0, 0)),
        ],
        out_specs=[
            pl.BlockSpec((tm, WIDTH), lambda i, j, tab: (i, tab[1, j])),
            pl.BlockSpec((tm, WIDTH), lambda i, j, tab: (i, tab[2, j])),
            pl.BlockSpec((tm * HEADS, HEAD_DIM), lambda i, j, tab: (i, 0)),
            pl.BlockSpec((tm * HEADS, HEAD_DIM), lambda i, j, tab: (i, 0)),
        ],
        scratch_shapes=[pltpu.VMEM((tm, D_MODEL), BF16)],
    )
    return pl.pallas_call(
        _in_proj_kernel,
        grid_spec=grid_spec,
        out_shape=[jax.ShapeDtypeStruct((t, N_P32 * WIDTH), F32),
                   jax.ShapeDtypeStruct((t, N_P16 * WIDTH), BF16),
                   jax.ShapeDtypeStruct((t * HEADS, HEAD_DIM), F32),
                   jax.ShapeDtypeStruct((t * HEADS, HEAD_DIM), F32)],
        compiler_params=_params("arbitrary", "arbitrary"),
        name="in_proj",
    )(table, h_all, norm_g.reshape(1, D_MODEL), w_bf, q_g.reshape(1, HEAD_DIM), k_g.reshape(1, HEAD_DIM))


def _pool_kernel(*refs, nseq, seq, pos0, has_prefix):
    if has_prefix:
        u_ref, p_ref, w_ref, s_ref, o_ref = refs
    else:
        u_ref, w_ref, s_ref, o_ref = refs
    g = pl.program_id(1)
    u = u_ref[...].reshape(nseq, seq, POOL_GW)
    head = p_ref[...] if has_prefix else jnp.zeros((nseq, POOL_MAX, POOL_GW), F32)
    x = jnp.concatenate([head, u], axis=1)
    pos = pos0 + lax.broadcasted_iota(jnp.int32, (1, seq, 1), 1)

    for gi, win in enumerate(POOL_WINDOWS):
        @pl.when(g == gi)
        def _(win=win):
            s = x
            sh = 1
            while sh < win:
                s = s + pltpu.roll(s, sh, 1)
                sh *= 2
            cnt = jnp.minimum(pos + 1, win).astype(F32)
            diff = (s[:, POOL_MAX:] / cnt - u).reshape(nseq * seq, POOL_GW)
            y = jnp.dot(diff.astype(BF16), w_ref[0], preferred_element_type=F32)
            o_ref[...] = (y * s_ref[...]).astype(o_ref.dtype)


def _pool(proj32, prefix16, w_pool_bf, scale, *, nb, seq, nseq, row0, pos0):
    rows = nseq * seq
    rb0 = row0 // rows
    has_prefix = prefix16 is not None
    ua0 = P32_UA * len(POOL_WINDOWS)
    in_specs = [pl.BlockSpec((rows, POOL_GW), lambda b, g: (rb0 + b, ua0 + g))]
    args = [proj32]
    if has_prefix:
        in_specs.append(pl.BlockSpec((nseq, POOL_MAX, POOL_GW), lambda b, g: (b, 0, g)))
        args.append(prefix16)
    in_specs += [
        pl.BlockSpec((1, POOL_GW, POOL_GW), lambda b, g: (g, 0, 0)),
        pl.BlockSpec((1, POOL_GW), lambda b, g: (0, g)),
    ]
    args += [w_pool_bf, scale.reshape(1, WIDTH)]
    return pl.pallas_call(
        functools.partial(_pool_kernel, nseq=nseq, seq=seq, pos0=pos0, has_prefix=has_prefix),
        grid=(nb // nseq, len(POOL_WINDOWS)),
        in_specs=in_specs,
        out_specs=pl.BlockSpec((rows, POOL_GW), lambda b, g: (b, g)),
        out_shape=jax.ShapeDtypeStruct((nb * seq, WIDTH), BF16),
        compiler_params=_params("parallel", "parallel"),
        name="pool",
    )(*args)


def _tri_after(n):
    r = lax.broadcasted_iota(jnp.int32, (n, n), 0)
    c = lax.broadcasted_iota(jnp.int32, (n, n), 1)
    return jnp.where(r > c, 1.0, 0.0).astype(BF16)


def _causal_mask(tq, tk, shift):
    r = lax.broadcasted_iota(jnp.int32, (tq, tk), 0)
    c = lax.broadcasted_iota(jnp.int32, (tq, tk), 1)
    return c < r + shift


def _sb_heads(q_ref, key, val, tk, tri, carry_ref, keep_ref, logit_ref, o_ref, mask=None):
    tq = q_ref.shape[0]
    first = mask is not None
    for h in range(HEADS):
        rows = slice(h * tq, (h + 1) * tq)
        z = lax.dot_general(q_ref[:, _head(h)], key(h), (((1,), (1,)), ((), ())),
                            preferred_element_type=F32)
        soft = jnp.maximum(z, 0.0) + jnp.log2(1.0 + jnp.exp2(-jnp.abs(z)))
        keep = -soft
        if first:
            keep = jnp.where(mask, keep, 0.0)
            logit_ref[rows, :tk] = z - soft
            carry_ref[h] = jnp.sum(keep, axis=1, keepdims=True)
        else:
            carry = carry_ref[h]
            logit_ref[rows, :tk] = z - soft + carry
            carry_ref[h] = carry + jnp.sum(keep, axis=1, keepdims=True)
        keep_ref[rows, :tk] = keep.astype(BF16)
    later = jnp.dot(keep_ref[:, :tk], tri, preferred_element_type=F32)
    for h in range(HEADS):
        rows = slice(h * tq, (h + 1) * tq)
        wts = jnp.exp2(logit_ref[rows, :tk] + later[rows])
        if first:
            wts = jnp.where(mask, wts, 0.0)
        pv = jnp.dot(wts.astype(BF16), val(h), preferred_element_type=F32)
        if first:
            o_ref[:, _head(h)] = pv
        else:
            o_ref[:, _head(h)] += pv


def _attn_scratch(tq):
    return [pltpu.VMEM((tq, WIDTH), F32),
            pltpu.VMEM((TK_WIDE, TK_WIDE), BF16),
            pltpu.VMEM((HEADS, tq, 1), F32),
            pltpu.VMEM((HEADS * tq, TK_WIDE), BF16),
            pltpu.VMEM((HEADS * tq, TK_WIDE), F32)]


def _attn_prompt_kernel(q_ref, k_ref, v_ref, o_ref, acc_ref, triw_ref, carry_ref, keep_ref, logit_ref):
    i = pl.program_id(1)
    tq = q_ref.shape[0]
    triw_ref[...] = _tri_after(TK_WIDE)
    d0 = pl.multiple_of(i * tq, tq)
    odd = jnp.bitwise_and(i, 1)

    def sweep(k0, tk, tri, mask=None):
        _sb_heads(q_ref, lambda h: k_ref[pl.ds(k0, tk), _head(h)], lambda h: v_ref[pl.ds(k0, tk), _head(h)],
                  tk, tri, carry_ref, keep_ref, logit_ref, acc_ref, mask)

    @pl.when(odd == 0)
    def _():
        sweep(d0, tq, _tri_after(tq), _causal_mask(tq, tq, 0))

    @pl.when(odd == 1)
    def _():
        sweep(pl.multiple_of(d0 - tq, tq), TK_WIDE, triw_ref[...], _causal_mask(tq, TK_WIDE, tq))

    first_wide = d0 - odd * tq

    @pl.loop(0, lax.shift_right_logical(i, 1))
    def _(t):
        sweep(pl.multiple_of(first_wide - (t + 1) * TK_WIDE, tq), TK_WIDE, triw_ref[...])

    o_ref[...] = acc_ref[...].astype(o_ref.dtype)


def _attn_prompt(proj16, *, nb, seq):
    nq = seq // LANE
    return pl.pallas_call(
        _attn_prompt_kernel,
        grid=(nb, nq),
        in_specs=[
            pl.BlockSpec((LANE, WIDTH), lambda b, i: (b * nq + i, P16_QB)),
            pl.BlockSpec((seq, WIDTH), lambda b, i: (b, P16_KB)),
            pl.BlockSpec((seq, WIDTH), lambda b, i: (b, P16_VB)),
        ],
        out_specs=pl.BlockSpec((LANE, WIDTH), lambda b, i: (b * nq + i, 0)),
        out_shape=jax.ShapeDtypeStruct((nb * seq, WIDTH), BF16),
        scratch_shapes=_attn_scratch(LANE),
        compiler_params=_params("parallel", "arbitrary"),
        name="attn_prompt",
    )(proj16, proj16, proj16)


def _attn_sample_kernel(q_ref, kn_ref, vn_ref, kp_ref, vp_ref, o_ref, acc_ref, triw_ref, carry_ref, keep_ref,
                        logit_ref, *, past):
    tq = q_ref.shape[0]
    triw_ref[...] = _tri_after(TK_WIDE)
    n_past = past // TK_WIDE

    _sb_heads(q_ref, lambda h: kn_ref[:, _head(h)], lambda h: vn_ref[:, _head(h)],
              tq, _tri_after(tq), carry_ref, keep_ref, logit_ref, acc_ref, _causal_mask(tq, tq, 0))

    @pl.loop(0, n_past)
    def _(t):
        k0 = (n_past - 1 - t) * TK_WIDE

        def head_rows(ref, h):
            rows = pl.ds(pl.multiple_of(k0 * HEADS, HEADS) + h, TK_WIDE, stride=HEADS)
            return ref[0, 0, rows, :].astype(BF16)

        _sb_heads(q_ref, lambda h: head_rows(kp_ref, h), lambda h: head_rows(vp_ref, h),
                  TK_WIDE, triw_ref[...], carry_ref, keep_ref, logit_ref, acc_ref)

    o_ref[...] = acc_ref[...].astype(o_ref.dtype)


def _attn_sample(proj16, cache_k, cache_v, layer, *, nb, seq, row0):
    rows = cache_k.shape[2]
    rb0 = row0 // seq
    return pl.pallas_call(
        functools.partial(_attn_sample_kernel, past=rows // HEADS),
        grid=(nb,),
        in_specs=[
            pl.BlockSpec((seq, WIDTH), lambda b: (rb0 + b, P16_QB)),
            pl.BlockSpec((seq, WIDTH), lambda b: (rb0 + b, P16_KB)),
            pl.BlockSpec((seq, WIDTH), lambda b: (rb0 + b, P16_VB)),
            pl.BlockSpec((1, 1, rows, HEAD_DIM), lambda b: (layer, b, 0, 0)),
            pl.BlockSpec((1, 1, rows, HEAD_DIM), lambda b: (layer, b, 0, 0)),
        ],
        out_specs=pl.BlockSpec((seq, WIDTH), lambda b: (b, 0)),
        out_shape=jax.ShapeDtypeStruct((nb * seq, WIDTH), BF16),
        scratch_shapes=_attn_scratch(seq),
        compiler_params=_params("parallel"),
        name="attn_sample",
    )(proj16, proj16, proj16, cache_k, cache_v)


def _group_roll(x, d):
    n, lanes = x.shape
    return pltpu.roll(x.reshape(n // SUBLANES, SUBLANES, lanes), d, 1).reshape(n, lanes)


def _cumsum_rows(x, sub):
    n = x.shape[0]
    sh = 1
    while sh < SUBLANES:
        x = x + jnp.where(sub >= sh, _group_roll(x, sh), 0.0)
        sh *= 2
    parts, run = [], None
    for g in range(n // SUBLANES):
        blk = x[g * SUBLANES:(g + 1) * SUBLANES]
        if run is not None:
            blk = blk + run
        parts.append(blk)
        run = blk[SUBLANES - 1:SUBLANES]
    return jnp.concatenate(parts, axis=0)


def _half_end_rows(x, half):
    n, lanes = x.shape
    blk = 2 * half
    if blk <= SUBLANES:
        x3 = x.reshape(n // SUBLANES, SUBLANES, lanes)
        parts = [jnp.broadcast_to(x3[:, s + half - 1:s + half, :], (n // SUBLANES, blk, lanes))
                 for s in range(0, SUBLANES, blk)]
        return jnp.concatenate(parts, axis=1).reshape(n, lanes)
    return jnp.concatenate([jnp.broadcast_to(x[s + half - 1:s + half], (blk, lanes))
                            for s in range(0, n, blk)], axis=0)


def _pair_classes(n):
    r = lax.broadcasted_iota(jnp.int32, (n, n), 0)
    c = lax.broadcasted_iota(jnp.int32, (n, n), 1)

    def block_of(x, size):
        return jnp.bitwise_and(x, -size)

    same = block_of(r, HG_SUB) == block_of(c, HG_SUB)
    delta = jnp.where(jnp.logical_and(same, c <= r), r - c, -1)
    level = jnp.full((n, n), -1, jnp.int32)
    half, m = HG_SUB, 0
    while half < n:
        blk = 2 * half
        hit = jnp.logical_and(block_of(r, blk) == block_of(c, blk),
                              jnp.logical_and(jnp.bitwise_and(r, half) != 0, jnp.bitwise_and(c, half) == 0))
        level = jnp.where(hit, m, level)
        half, m = blk, m + 1
    return delta, level


def _hgrn_kernel(*refs, layer, valid_len, has_state):
    refs = list(refs)
    z_ref, q_ref, v_ref, lbp_ref, ng_ref = refs[:5]
    s0_ref = refs[5] if has_state else None
    o_ref, sf_ref, st_ref, pair_ref, lvl_ref, att_ref, qdec_ref, kdec_ref, dec_ref = refs[-9:]
    c = pl.program_id(1)
    nc = pl.num_programs(1)
    n = HG_CHUNK

    lbp = lbp_ref[...]
    e = jnp.exp(lbp - jnp.max(lbp, axis=0, keepdims=True))
    soft = e / jnp.sum(e, axis=0, keepdims=True)
    run = soft[0:1]
    for l in range(1, layer + 1):
        run = run + soft[l:l + 1]
    lb_all = run - soft[0:1]

    row = lax.broadcasted_iota(jnp.int32, (n, 1), 0)
    valid = None if valid_len is None else (c * n + row) < valid_len
    sub = jnp.bitwise_and(row, SUBLANES - 1)
    delta, level = _pair_classes(n)
    is_delta = [delta == d for d in range(HG_SUB)]
    halves = []
    half = HG_SUB
    while half < n:
        halves.append(half)
        half *= 2
    is_level = [level == m for m in range(len(halves))]
    is_second = [jnp.bitwise_and(row, half) != 0 for half in halves]

    @pl.when(c == 0)
    def _():
        for h in range(HEADS):
            st_ref[h] = s0_ref[0, 0, h].T if has_state else jnp.zeros((HEAD_DIM, HEAD_DIM), F32)

    for h in range(HEADS):
        hs = _head(h)
        rows = slice(h * n, (h + 1) * n)
        z = z_ref[:, hs] * LOG2E
        qc = q_ref[:, hs]
        lb = lb_all[:, hs]
        ez = jnp.exp2(-jnp.abs(z))
        a = jnp.log2(jnp.maximum(lb, LB_FLOOR))
        b = jnp.log1p(-lb) * LOG2E + (jnp.minimum(z, 0.0) - jnp.log2(1.0 + ez))
        log_f = jnp.maximum(a, b) + jnp.log2(1.0 + jnp.exp2(-jnp.abs(a - b)))
        kk = (1.0 - lb) * (jnp.where(z >= 0.0, ez, 1.0) / (1.0 + ez))
        if valid_len is not None:
            log_f = jnp.where(valid, log_f, 0.0)
            kk = jnp.where(valid, kk, 0.0)
        qs = qc * _sigmoid(qc)
        cum = _cumsum_rows(log_f, sub)

        att = jnp.where(is_delta[0], jnp.sum(qs * kk, axis=-1, keepdims=True), 0.0)
        for d in range(1, HG_SUB):
            w = jnp.exp2(jnp.minimum(cum - _group_roll(cum, d), 0.0))
            a_d = jnp.sum(qs * _group_roll(kk, d) * w, axis=-1, keepdims=True)
            att = jnp.where(is_delta[d], a_d, att)
        pair_ref[h] = att

        for m, half in enumerate(halves):
            x = jnp.where(is_second[m], qs, kk) * jnp.exp2(-jnp.abs(cum - _half_end_rows(cum, half)))
            lvl_ref[m, rows] = x.astype(BF16)

        last = cum[n - 1:n]
        qdec_ref[rows] = (qs * jnp.exp2(cum)).astype(BF16)
        kdec_ref[rows] = (kk * jnp.exp2(last - cum)).astype(BF16)
        dec_ref[h] = jnp.exp2(last)

    for h in range(HEADS):
        rows = slice(h * n, (h + 1) * n)
        att = pair_ref[h]
        for m in range(len(halves)):
            x = lvl_ref[m, rows]
            a_m = lax.dot_general(x, x, (((1,), (1,)), ((), ())), preferred_element_type=F32)
            att = jnp.where(is_level[m], a_m, att)
        att_ref[h] = att.astype(BF16)

    for h in range(HEADS):
        hs = _head(h)
        rows = slice(h * n, (h + 1) * n)
        v_bf = v_ref[:, hs]
        st = st_ref[h]
        o = jnp.dot(att_ref[h], v_bf, preferred_element_type=F32)
        o = o + lax.dot_general(qdec_ref[rows], st.astype(BF16), (((1,), (1,)), ((), ())),
                                preferred_element_type=F32)
        st_ref[h] = st * dec_ref[h] + lax.dot_general(v_bf, kdec_ref[rows], (((0,), (0,)), ((), ())),
                                                      preferred_element_type=F32)
        ms = jnp.mean(o * o, axis=-1, keepdims=True)
        o_ref[:, hs] = (o * lax.rsqrt(ms + EPS) * ng_ref[...]).astype(o_ref.dtype)

    @pl.when(c == nc - 1)
    def _():
        for h in range(HEADS):
            sf_ref[0, h] = st_ref[h].T


def _hgrn(proj32, proj16, lb_param, norm_g, state0, layer, *, nb, seq, row0, valid_len):
    n = HG_CHUNK
    nc = seq // n
    rb0 = row0 // n
    has_state = state0 is not None
    depth = lb_param.shape[0]
    n_levels = (n // HG_SUB).bit_length() - 1

    def col(cidx):
        return pl.BlockSpec((n, WIDTH), lambda b, c: (rb0 + b * nc + c, cidx))

    in_specs = [col(P32_FC), col(P32_QC), col(P16_IC),
                pl.BlockSpec((depth, WIDTH), lambda b, c: (0, 0)),
                pl.BlockSpec((1, HEAD_DIM), lambda b, c: (0, 0))]
    args = [proj32, proj32, proj16, lb_param, norm_g.reshape(1, HEAD_DIM)]
    if has_state:
        in_specs.append(pl.BlockSpec((1, 1, HEADS, HEAD_DIM, HEAD_DIM), lambda b, c: (layer, b, 0, 0, 0)))
        args.append(state0)
    return pl.pallas_call(
        functools.partial(_hgrn_kernel, layer=layer, valid_len=valid_len, has_state=has_state),
        grid=(nb, nc),
        in_specs=in_specs,
        out_specs=[
            pl.BlockSpec((n, WIDTH), lambda b, c: (b * nc + c, 0)),
            pl.BlockSpec((1, HEADS, HEAD_DIM, HEAD_DIM), lambda b, c: (b, 0, 0, 0)),
        ],
        out_shape=[
            jax.ShapeDtypeStruct((nb * seq, WIDTH), BF16),
            jax.ShapeDtypeStruct((nb, HEADS, HEAD_DIM, HEAD_DIM), F32),
        ],
        scratch_shapes=[
            pltpu.VMEM((HEADS, HEAD_DIM, HEAD_DIM), F32),
            pltpu.VMEM((HEADS, n, n), F32),
            pltpu.VMEM((n_levels, HEADS * n, HEAD_DIM), BF16),
            pltpu.VMEM((HEADS, n, n), BF16),
            pltpu.VMEM((HEADS * n, HEAD_DIM), BF16),
            pltpu.VMEM((HEADS * n, HEAD_DIM), BF16),
            pltpu.VMEM((HEADS, 1, HEAD_DIM), F32),
        ],
        compiler_params=_params("parallel", "arbitrary"),
        name="hgrn",
    )(*args)


def _merge_kernel(h_ref, yap_ref, ybp_ref, ycp_ref, yas_ref, ybs_ref, ycs_ref, ga_ref, gb_ref, gc_ref,
                  ma_ref, mb_ref, mc_ref, wb_ref, wo_ref, *o_refs, prompt_tiles):
    is_prompt = pl.program_id(0) < prompt_tiles
    merged = None
    for n, (yp_ref, ys_ref, g_ref, m_ref) in enumerate(((yap_ref, yas_ref, ga_ref, ma_ref),
                                                        (ybp_ref, ybs_ref, gb_ref, mb_ref),
                                                        (ycp_ref, ycs_ref, gc_ref, mc_ref))):
        y = jnp.where(is_prompt, yp_ref[...], ys_ref[...]).astype(F32)
        y = (y * g_ref[...].astype(F32)).astype(BF16)
        p = m_ref[...].astype(F32) * jnp.dot(y, wb_ref[n], preferred_element_type=F32)
        merged = p if merged is None else merged + p
    out = h_ref[...] + jnp.dot(merged.astype(BF16), wo_ref[...], preferred_element_type=F32)
    if len(o_refs) == 1:
        o_refs[0][...] = out
    else:
        @pl.when(is_prompt)
        def _():
            o_refs[0][...] = out

        @pl.when(jnp.logical_not(is_prompt))
        def _():
            o_refs[1][...] = out


def _merge(h_all, y_prompt, y_sample, proj16, wb_bf, wo_bf, split):
    t = h_all.shape[0]
    tp, ts = y_prompt[0].shape[0], y_sample[0].shape[0]
    tm = next(c for c in (256, 128) if tp % c == 0 and ts % c == 0)
    assert tp + ts == t
    n_p = tp // tm
    row = pl.BlockSpec((tm, WIDTH), lambda i: (i, 0))
    prow = pl.BlockSpec((tm, WIDTH), lambda i: (jnp.minimum(i, n_p - 1), 0))
    srow = pl.BlockSpec((tm, WIDTH), lambda i: (jnp.maximum(i - n_p, 0), 0))

    def col(cidx):
        return pl.BlockSpec((tm, WIDTH), lambda i: (i, cidx))

    if split:
        out_specs = [prow, srow]
        out_shape = [jax.ShapeDtypeStruct((tp, D_MODEL), F32), jax.ShapeDtypeStruct((ts, D_MODEL), F32)]
    else:
        out_specs = row
        out_shape = jax.ShapeDtypeStruct((t, D_MODEL), F32)
    return pl.pallas_call(
        functools.partial(_merge_kernel, prompt_tiles=n_p),
        grid=(t // tm,),
        in_specs=[row, prow, prow, prow, srow, srow, srow,
                  col(P16_GA), col(P16_GB), col(P16_GC), col(P16_MA), col(P16_MB), col(P16_MC),
                  pl.BlockSpec((3, WIDTH, D_MODEL), lambda i: (0, 0, 0)),
                  pl.BlockSpec((D_MODEL, D_MODEL), lambda i: (0, 0))],
        out_specs=out_specs,
        out_shape=out_shape,
        compiler_params=_params("arbitrary"),
        name="merge",
    )(h_all, *y_prompt, *y_sample, *([proj16] * 6), wb_bf, wo_bf)


def kernel(x_prompt, x_sample, cache_k, cache_v, state_pool, state_hgrn, meta_tokens, norm_g, w_in,
           q_norm_g, k_norm_g, w_pool, pool_scale, hgrn_lower_bounds, hgrn_norm_g, w_branch, w_out):
    bp, seq, _ = x_prompt.shape
    bs, dseq, _ = x_sample.shape
    depth = w_in.shape[0]
    past = cache_k.shape[2]
    lp_real = N_META + seq
    lp = -(-lp_real // LANE) * LANE
    tp = bp * lp
    ts = bs * dseq

    meta = jnp.broadcast_to(meta_tokens.astype(F32)[None], (bp, N_META, D_MODEL))
    hp = jnp.concatenate([meta, x_prompt, jnp.zeros((bp, lp - lp_real, D_MODEL), F32)], axis=1)
    h_all = jnp.concatenate([hp.reshape(tp, D_MODEL), x_sample.reshape(ts, D_MODEL)], axis=0)

    cache_k2 = cache_k.reshape(depth, bs, past * HEADS, HEAD_DIM)
    cache_v2 = cache_v.reshape(depth, bs, past * HEADS, HEAD_DIM)
    prefix16 = jnp.pad(state_pool, ((0, 0), (0, 0), (1, 0), (0, 0)))

    kp, vp, pp, sp, ksm, vsm, psm, ssm = [], [], [], [], [], [], [], []
    for l in range(depth):
        proj32, proj16, keys, vals = _in_proj(h_all, norm_g[l], w_in[l].astype(BF16), q_norm_g[l], k_norm_g[l])
        w_pool_bf = w_pool[l].astype(BF16)

        ya_p = _pool(proj32, None, w_pool_bf, pool_scale[l], nb=bp, seq=lp, nseq=1, row0=0, pos0=0)
        yb_p = _attn_prompt(proj16, nb=bp, seq=lp)
        yc_p, st_p = _hgrn(proj32, proj16, hgrn_lower_bounds, hgrn_norm_g[l], None, l,
                           nb=bp, seq=lp, row0=0, valid_len=lp_real)
        ya_s = _pool(proj32, prefix16[l], w_pool_bf, pool_scale[l], nb=bs, seq=dseq,
                     nseq=math.gcd(bs, tp // dseq), row0=tp, pos0=past)
        yb_s = _attn_sample(proj16, cache_k2, cache_v2, l, nb=bs, seq=dseq, row0=tp)
        yc_s, st_s = _hgrn(proj32, proj16, hgrn_lower_bounds, hgrn_norm_g[l], state_hgrn, l,
                           nb=bs, seq=dseq, row0=tp, valid_len=None)

        h_all = _merge(h_all, (ya_p, yb_p, yc_p), (ya_s, yb_s, yc_s), proj16,
                       w_branch[l].astype(BF16), w_out[l].astype(BF16), split=(l == depth - 1))

        def pool_rows(r0, r1):
            return proj32[r0:r1, P32_UA * WIDTH:(P32_UA + 1) * WIDTH]

        def token_rows(a, r0, r1):
            return a[r0 * HEADS:r1 * HEADS].reshape(r1 - r0, HEADS, HEAD_DIM)

        kp += [token_rows(keys, b * lp, b * lp + lp_real) for b in range(bp)]
        vp += [token_rows(vals, b * lp, b * lp + lp_real) for b in range(bp)]
        pp.append(jnp.stack([pool_rows(b * lp + lp_real - (POOL_MAX - 1), b * lp + lp_real) for b in range(bp)]))
        sp.append(st_p)
        ksm.append(keys[tp * HEADS:].reshape(bs, dseq, HEADS, HEAD_DIM))
        vsm.append(vals[tp * HEADS:].reshape(bs, dseq, HEADS, HEAD_DIM))
        full_s = jnp.concatenate([state_pool[l], pool_rows(tp, tp + ts).reshape(bs, dseq, WIDTH)], axis=1)
        psm.append(full_s[:, -(POOL_MAX - 1):])
        ssm.append(st_s)

    h_prompt, h_sample = h_all
    y_prompt = h_prompt.reshape(bp, lp, D_MODEL)[:, N_META:lp_real]
    y_sample = h_sample.reshape(bs, dseq, D_MODEL)
    k_prompt = jnp.stack(kp).reshape(depth, bp, lp_real, HEADS, HEAD_DIM)
    v_prompt = jnp.stack(vp).reshape(depth, bp, lp_real, HEADS, HEAD_DIM)
    return (y_prompt, y_sample, k_prompt, v_prompt, jnp.stack(pp), jnp.stack(sp),
            jnp.stack(ksm), jnp.stack(vsm), jnp.stack(psm), jnp.stack(ssm))
```

```python
import functools
import math

import jax
import jax.numpy as jnp
from jax import lax
from jax.experimental import pallas as pl
from jax.experimental.pallas import tpu as pltpu

F32 = jnp.float32
BF16 = jnp.bfloat16

D_MODEL = 1024
WIDTH = 1024
N_SPLIT = 13
N_META = 16
POOL_WINDOWS = (2, 4, 8, 16)
POOL_GW = WIDTH // len(POOL_WINDOWS)
POOL_MAX = max(POOL_WINDOWS)
HEADS = 8
HEAD_DIM = WIDTH // HEADS
SB_SCALE = HEAD_DIM ** -0.5
LOG2E = 1.4426950408889634
EPS = 1e-6
LB_FLOOR = 1e-30
HG_CHUNK = 64
SUBLANES = 8
HG_SUB = 4
LANE = 128
TK_WIDE = 256
VMEM_LIMIT = 56 * 1024 * 1024

C_UA, C_GA, C_QB, C_KB, C_VB, C_GB, C_FC, C_QC, C_IC, C_GC, C_MA, C_MB, C_MC = range(N_SPLIT)


def _head(h):
    return slice(h * HEAD_DIM, (h + 1) * HEAD_DIM)


def _sigmoid(x):
    return 1.0 / (1.0 + jnp.exp(-x))


def _params(*sem):
    return pltpu.CompilerParams(dimension_semantics=sem, vmem_limit_bytes=VMEM_LIMIT)


STEP_GROUPS = (C_UA, C_KB, C_VB, C_FC, C_QC, C_QB, C_IC, C_GA, C_GB, C_GC, C_MA, C_MB, C_MC)
P32_UA, P32_FC, P32_QC = range(3)
N_P32 = 3
P16_KB, P16_VB, P16_KC, P16_QB, P16_IC, P16_GA, P16_GB, P16_GC, P16_MA, P16_MB, P16_MC = range(11)
N_P16 = 11
STEP_P32 = (0, 0, 0, 1, 2, 2, 2, 2, 2, 2, 2, 2, 2)
STEP_P16 = (0, 0, 1, 2, 2, 3, 4, 5, 6, 7, 8, 9, 10)
S_UA, S_KB, S_VB, S_FC, S_QC, S_QB, S_IC, S_GA = range(8)
S_MA = S_GA + 3


def _in_proj_kernel(tab_ref, x_ref, g_ref, w_ref, qg_ref, kg_ref, lbp_ref, o32_ref, o16_ref, kh_ref, vh_ref,
                    xn_ref, *, layer):
    j = pl.program_id(1)
    tm = x_ref.shape[0]

    @pl.when(j == 0)
    def _():
        x = x_ref[...]
        ms = jnp.mean(x * x, axis=-1, keepdims=True)
        xn_ref[...] = (x * lax.rsqrt(ms + EPS) * g_ref[...]).astype(BF16)

    def project():
        return jnp.dot(xn_ref[...], w_ref[...], preferred_element_type=F32)

    def head_norm(a, gain):
        ms = jnp.mean(a * a, axis=-1, keepdims=True)
        return a * lax.rsqrt(ms + EPS) * gain

    def head_rows(h):
        return pl.ds(h, tm, stride=HEADS)

    @pl.when(j == S_UA)
    def _():
        o32_ref[...] = project()

    @pl.when(j == S_FC)
    def _():
        lbp = lbp_ref[...]
        e = jnp.exp(lbp - jnp.max(lbp, axis=0, keepdims=True))
        soft = e / jnp.sum(e, axis=0, keepdims=True)
        run = soft[0:1]
        for l in range(1, layer + 1):
            run = run + soft[l:l + 1]
        lb = run - soft[0:1]
        z = project() * LOG2E
        ez = jnp.exp2(-jnp.abs(z))
        a = jnp.log2(jnp.maximum(lb, LB_FLOOR))
        b = jnp.log1p(-lb) * LOG2E + (jnp.minimum(z, 0.0) - jnp.log2(1.0 + ez))
        o32_ref[...] = jnp.maximum(a, b) + jnp.log2(1.0 + jnp.exp2(-jnp.abs(a - b)))
        o16_ref[...] = ((1.0 - lb) * (jnp.where(z >= 0.0, ez, 1.0) / (1.0 + ez))).astype(BF16)

    @pl.when(j == S_QC)
    def _():
        q = project()
        o32_ref[...] = q * _sigmoid(q)

    @pl.when(j == S_KB)
    def _():
        acc = project()
        for h in range(HEADS):
            kn = head_norm(acc[:, _head(h)], kg_ref[...])
            kh_ref[head_rows(h), :] = kn
            o16_ref[:, _head(h)] = kn.astype(BF16)

    @pl.when(j == S_VB)
    def _():
        acc = project()
        for h in range(HEADS):
            vh_ref[head_rows(h), :] = acc[:, _head(h)]
        o16_ref[...] = acc.astype(BF16)

    @pl.when(j == S_QB)
    def _():
        acc = project()
        for h in range(HEADS):
            qn = head_norm(acc[:, _head(h)], qg_ref[...])
            o16_ref[:, _head(h)] = (qn * (SB_SCALE * LOG2E)).astype(BF16)

    @pl.when(j == S_IC)
    def _():
        o16_ref[...] = project().astype(BF16)

    @pl.when(jnp.logical_and(j >= S_GA, j < S_MA))
    def _():
        g = project()
        o16_ref[...] = (g * _sigmoid(g)).astype(BF16)

    @pl.when(j >= S_MA)
    def _():
        o16_ref[...] = _sigmoid(project()).astype(BF16)


def _pick_tile(t, candidates):
    return next(c for c in candidates if t % c == 0)


def _in_proj(h_all, norm_g, w_bf, q_g, k_g, lb_param, layer):
    t = h_all.shape[0]
    tm = _pick_tile(t, (1024, 512, 256, 128))
    depth = lb_param.shape[0]
    table = jnp.array([STEP_GROUPS, STEP_P32, STEP_P16], jnp.int32)
    grid_spec = pltpu.PrefetchScalarGridSpec(
        num_scalar_prefetch=1,
        grid=(t // tm, N_SPLIT),
        in_specs=[
            pl.BlockSpec((tm, D_MODEL), lambda i, j, tab: (i, 0)),
            pl.BlockSpec((1, D_MODEL), lambda i, j, tab: (0, 0)),
            pl.BlockSpec((D_MODEL, WIDTH), lambda i, j, tab: (0, tab[0, j])),
            pl.BlockSpec((1, HEAD_DIM), lambda i, j, tab: (0, 0)),
            pl.BlockSpec((1, HEAD_DIM), lambda i, j, tab: (0, 0)),
            pl.BlockSpec((depth, WIDTH), lambda i, j, tab: (0, 0)),
        ],
        out_specs=[
            pl.BlockSpec((tm, WIDTH), lambda i, j, tab: (i, tab[1, j])),
            pl.BlockSpec((tm, WIDTH), lambda i, j, tab: (i, tab[2, j])),
            pl.BlockSpec((tm * HEADS, HEAD_DIM), lambda i, j, tab: (i, 0)),
            pl.BlockSpec((tm * HEADS, HEAD_DIM), lambda i, j, tab: (i, 0)),
        ],
        scratch_shapes=[pltpu.VMEM((tm, D_MODEL), BF16)],
    )
    return pl.pallas_call(
        functools.partial(_in_proj_kernel, layer=layer),
        grid_spec=grid_spec,
        out_shape=[jax.ShapeDtypeStruct((t, N_P32 * WIDTH), F32),
                   jax.ShapeDtypeStruct((t, N_P16 * WIDTH), BF16),
                   jax.ShapeDtypeStruct((t * HEADS, HEAD_DIM), F32),
                   jax.ShapeDtypeStruct((t * HEADS, HEAD_DIM), F32)],
        compiler_params=_params("arbitrary", "arbitrary"),
        name="in_proj",
    )(table, h_all, norm_g.reshape(1, D_MODEL), w_bf, q_g.reshape(1, HEAD_DIM), k_g.reshape(1, HEAD_DIM),
      lb_param)


def _pool_kernel(*refs, nseq, seq, pos0, has_prefix):
    if has_prefix:
        u_ref, p_ref, w_ref, s_ref, o_ref = refs
    else:
        u_ref, w_ref, s_ref, o_ref = refs
    g = pl.program_id(1)
    u = u_ref[...].reshape(nseq, seq, POOL_GW)
    head = p_ref[...] if has_prefix else jnp.zeros((nseq, POOL_MAX, POOL_GW), F32)
    x = jnp.concatenate([head, u], axis=1)
    pos = pos0 + lax.broadcasted_iota(jnp.int32, (1, seq, 1), 1)

    for gi, win in enumerate(POOL_WINDOWS):
        @pl.when(g == gi)
        def _(win=win):
            s = x
            sh = 1
            while sh < win:
                s = s + pltpu.roll(s, sh, 1)
                sh *= 2
            cnt = jnp.minimum(pos + 1, win).astype(F32)
            diff = (s[:, POOL_MAX:] / cnt - u).reshape(nseq * seq, POOL_GW)
            y = jnp.dot(diff.astype(BF16), w_ref[0], preferred_element_type=F32)
            o_ref[...] = (y * s_ref[...]).astype(o_ref.dtype)


def _pool(proj32, prefix16, w_pool_bf, scale, *, nb, seq, nseq, row0, pos0):
    rows = nseq * seq
    rb0 = row0 // rows
    has_prefix = prefix16 is not None
    ua0 = P32_UA * len(POOL_WINDOWS)
    in_specs = [pl.BlockSpec((rows, POOL_GW), lambda b, g: (rb0 + b, ua0 + g))]
    args = [proj32]
    if has_prefix:
        in_specs.append(pl.BlockSpec((nseq, POOL_MAX, POOL_GW), lambda b, g: (b, 0, g)))
        args.append(prefix16)
    in_specs += [
        pl.BlockSpec((1, POOL_GW, POOL_GW), lambda b, g: (g, 0, 0)),
        pl.BlockSpec((1, POOL_GW), lambda b, g: (0, g)),
    ]
    args += [w_pool_bf, scale.reshape(1, WIDTH)]
    return pl.pallas_call(
        functools.partial(_pool_kernel, nseq=nseq, seq=seq, pos0=pos0, has_prefix=has_prefix),
        grid=(nb // nseq, len(POOL_WINDOWS)),
        in_specs=in_specs,
        out_specs=pl.BlockSpec((rows, POOL_GW), lambda b, g: (b, g)),
        out_shape=jax.ShapeDtypeStruct((nb * seq, WIDTH), BF16),
        compiler_params=_params("parallel", "parallel"),
        name="pool",
    )(*args)


def _tri_after(n):
    r = lax.broadcasted_iota(jnp.int32, (n, n), 0)
    c = lax.broadcasted_iota(jnp.int32, (n, n), 1)
    return jnp.where(r > c, 1.0, 0.0).astype(BF16)


def _causal_mask(tq, tk, shift):
    r = lax.broadcasted_iota(jnp.int32, (tq, tk), 0)
    c = lax.broadcasted_iota(jnp.int32, (tq, tk), 1)
    return c < r + shift


def _sb_heads(q_ref, key, val, tk, tri, carry_ref, keep_ref, logit_ref, o_ref, mask=None):
    tq = q_ref.shape[0]
    first = mask is not None
    for h in range(HEADS):
        rows = slice(h * tq, (h + 1) * tq)
        z = lax.dot_general(q_ref[:, _head(h)], key(h), (((1,), (1,)), ((), ())),
                            preferred_element_type=F32)
        soft = jnp.maximum(z, 0.0) + jnp.log2(1.0 + jnp.exp2(-jnp.abs(z)))
        if first:
            logit_ref[rows, :tk] = z - soft
            soft = jnp.where(mask, soft, 0.0)
            carry_ref[h] = jnp.sum(soft, axis=1, keepdims=True)
        else:
            carry = carry_ref[h]
            logit_ref[rows, :tk] = z - soft - carry
            carry_ref[h] = carry + jnp.sum(soft, axis=1, keepdims=True)
        keep_ref[rows, :tk] = soft.astype(BF16)
    later = jnp.dot(keep_ref[:, :tk], tri, preferred_element_type=F32)
    for h in range(HEADS):
        rows = slice(h * tq, (h + 1) * tq)
        wts = jnp.exp2(logit_ref[rows, :tk] - later[rows])
        if first:
            wts = jnp.where(mask, wts, 0.0)
        pv = jnp.dot(wts.astype(BF16), val(h), preferred_element_type=F32)
        if first:
            o_ref[:, _head(h)] = pv
        else:
            o_ref[:, _head(h)] += pv


def _attn_scratch(tq):
    return [pltpu.VMEM((tq, WIDTH), F32),
            pltpu.VMEM((TK_WIDE, TK_WIDE), BF16),
            pltpu.VMEM((HEADS, tq, 1), F32),
            pltpu.VMEM((HEADS * tq, TK_WIDE), BF16),
            pltpu.VMEM((HEADS * tq, TK_WIDE), F32)]


def _attn_prompt_kernel(q_ref, k_ref, v_ref, o_ref, acc_ref, triw_ref, carry_ref, keep_ref, logit_ref):
    i = pl.program_id(1)
    tq = q_ref.shape[0]
    triw_ref[...] = _tri_after(TK_WIDE)
    d0 = pl.multiple_of(i * tq, tq)
    odd = jnp.bitwise_and(i, 1)

    def sweep(k0, tk, tri, mask=None):
        _sb_heads(q_ref, lambda h: k_ref[pl.ds(k0, tk), _head(h)], lambda h: v_ref[pl.ds(k0, tk), _head(h)],
                  tk, tri, carry_ref, keep_ref, logit_ref, acc_ref, mask)

    @pl.when(odd == 0)
    def _():
        sweep(d0, tq, _tri_after(tq), _causal_mask(tq, tq, 0))

    @pl.when(odd == 1)
    def _():
        sweep(pl.multiple_of(d0 - tq, tq), TK_WIDE, triw_ref[...], _causal_mask(tq, TK_WIDE, tq))

    first_wide = d0 - odd * tq

    @pl.loop(0, lax.shift_right_logical(i, 1))
    def _(t):
        sweep(pl.multiple_of(first_wide - (t + 1) * TK_WIDE, tq), TK_WIDE, triw_ref[...])

    o_ref[...] = acc_ref[...].astype(o_ref.dtype)


def _attn_prompt(proj16, *, nb, seq):
    nq = seq // LANE
    return pl.pallas_call(
        _attn_prompt_kernel,
        grid=(nb, nq),
        in_specs=[
            pl.BlockSpec((LANE, WIDTH), lambda b, i: (b * nq + i, P16_QB)),
            pl.BlockSpec((seq, WIDTH), lambda b, i: (b, P16_KB)),
            pl.BlockSpec((seq, WIDTH), lambda b, i: (b, P16_VB)),
        ],
        out_specs=pl.BlockSpec((LANE, WIDTH), lambda b, i: (b * nq + i, 0)),
        out_shape=jax.ShapeDtypeStruct((nb * seq, WIDTH), BF16),
        scratch_shapes=_attn_scratch(LANE),
        compiler_params=_params("parallel", "arbitrary"),
        name="attn_prompt",
    )(proj16, proj16, proj16)


def _attn_sample_kernel(q_ref, kn_ref, vn_ref, kp_ref, vp_ref, o_ref, acc_ref, triw_ref, carry_ref, keep_ref,
                        logit_ref, *, past):
    tq = q_ref.shape[0]
    triw_ref[...] = _tri_after(TK_WIDE)
    n_past = past // TK_WIDE

    _sb_heads(q_ref, lambda h: kn_ref[:, _head(h)], lambda h: vn_ref[:, _head(h)],
              tq, _tri_after(tq), carry_ref, keep_ref, logit_ref, acc_ref, _causal_mask(tq, tq, 0))

    @pl.loop(0, n_past)
    def _(t):
        k0 = (n_past - 1 - t) * TK_WIDE

        def head_rows(ref, h):
            rows = pl.ds(pl.multiple_of(k0 * HEADS, HEADS) + h, TK_WIDE, stride=HEADS)
            return ref[0, 0, rows, :].astype(BF16)

        _sb_heads(q_ref, lambda h: head_rows(kp_ref, h), lambda h: head_rows(vp_ref, h),
                  TK_WIDE, triw_ref[...], carry_ref, keep_ref, logit_ref, acc_ref)

    o_ref[...] = acc_ref[...].astype(o_ref.dtype)


def _attn_sample(proj16, cache_k, cache_v, layer, *, nb, seq, row0):
    rows = cache_k.shape[2]
    rb0 = row0 // seq
    return pl.pallas_call(
        functools.partial(_attn_sample_kernel, past=rows // HEADS),
        grid=(nb,),
        in_specs=[
            pl.BlockSpec((seq, WIDTH), lambda b: (rb0 + b, P16_QB)),
            pl.BlockSpec((seq, WIDTH), lambda b: (rb0 + b, P16_KB)),
            pl.BlockSpec((seq, WIDTH), lambda b: (rb0 + b, P16_VB)),
            pl.BlockSpec((1, 1, rows, HEAD_DIM), lambda b: (layer, b, 0, 0)),
            pl.BlockSpec((1, 1, rows, HEAD_DIM), lambda b: (layer, b, 0, 0)),
        ],
        out_specs=pl.BlockSpec((seq, WIDTH), lambda b: (b, 0)),
        out_shape=jax.ShapeDtypeStruct((nb * seq, WIDTH), BF16),
        scratch_shapes=_attn_scratch(seq),
        compiler_params=_params("parallel"),
        name="attn_sample",
    )(proj16, proj16, proj16, cache_k, cache_v)


def _group_roll(x, d):
    n, lanes = x.shape
    return pltpu.roll(x.reshape(n // SUBLANES, SUBLANES, lanes), d, 1).reshape(n, lanes)


def _cumsum_rows(x, sub):
    n = x.shape[0]
    sh = 1
    while sh < SUBLANES:
        x = x + jnp.where(sub >= sh, _group_roll(x, sh), 0.0)
        sh *= 2
    parts, run = [], None
    for g in range(n // SUBLANES):
        blk = x[g * SUBLANES:(g + 1) * SUBLANES]
        if run is not None:
            blk = blk + run
        parts.append(blk)
        run = blk[SUBLANES - 1:SUBLANES]
    return jnp.concatenate(parts, axis=0)


def _half_end_rows(x, half):
    n, lanes = x.shape
    blk = 2 * half
    if blk <= SUBLANES:
        x3 = x.reshape(n // SUBLANES, SUBLANES, lanes)
        parts = [jnp.broadcast_to(x3[:, s + half - 1:s + half, :], (n // SUBLANES, blk, lanes))
                 for s in range(0, SUBLANES, blk)]
        return jnp.concatenate(parts, axis=1).reshape(n, lanes)
    return jnp.concatenate([jnp.broadcast_to(x[s + half - 1:s + half], (blk, lanes))
                            for s in range(0, n, blk)], axis=0)


def _pair_classes(n):
    r = lax.broadcasted_iota(jnp.int32, (n, n), 0)
    c = lax.broadcasted_iota(jnp.int32, (n, n), 1)

    def block_of(x, size):
        return jnp.bitwise_and(x, -size)

    same = block_of(r, HG_SUB) == block_of(c, HG_SUB)
    delta = jnp.where(jnp.logical_and(same, c <= r), r - c, -1)
    level = jnp.full((n, n), -1, jnp.int32)
    half, m = HG_SUB, 0
    while half < n:
        blk = 2 * half
        hit = jnp.logical_and(block_of(r, blk) == block_of(c, blk),
                              jnp.logical_and(jnp.bitwise_and(r, half) != 0, jnp.bitwise_and(c, half) == 0))
        level = jnp.where(hit, m, level)
        half, m = blk, m + 1
    return delta, level


def _hgrn_kernel(*refs, valid_len, has_state):
    refs = list(refs)
    f_ref, q_ref, k_ref, v_ref, ng_ref = refs[:5]
    s0_ref = refs[5] if has_state else None
    o_ref, sf_ref, st_ref, pair_ref, lvl_ref, att_ref, qdec_ref, kdec_ref, dec_ref = refs[-9:]
    c = pl.program_id(1)
    nc = pl.num_programs(1)
    n = HG_CHUNK

    row = lax.broadcasted_iota(jnp.int32, (n, 1), 0)
    valid = None if valid_len is None else (c * n + row) < valid_len
    sub = jnp.bitwise_and(row, SUBLANES - 1)
    delta, level = _pair_classes(n)
    is_delta = [delta == d for d in range(HG_SUB)]
    halves = []
    half = HG_SUB
    while half < n:
        halves.append(half)
        half *= 2
    is_level = [level == m for m in range(len(halves))]
    is_second = [jnp.bitwise_and(row, half) != 0 for half in halves]

    @pl.when(c == 0)
    def _():
        for h in range(HEADS):
            st_ref[h] = s0_ref[0, 0, h].T if has_state else jnp.zeros((HEAD_DIM, HEAD_DIM), F32)

    for h in range(HEADS):
        hs = _head(h)
        rows = slice(h * n, (h + 1) * n)
        log_f = f_ref[:, hs]
        kk = k_ref[:, hs].astype(F32)
        qs = q_ref[:, hs]
        if valid_len is not None:
            log_f = jnp.where(valid, log_f, 0.0)
            kk = jnp.where(valid, kk, 0.0)
        cum = _cumsum_rows(log_f, sub)

        att = jnp.where(is_delta[0], jnp.sum(qs * kk, axis=-1, keepdims=True), 0.0)
        for d in range(1, HG_SUB):
            w = jnp.exp2(jnp.minimum(cum - _group_roll(cum, d), 0.0))
            a_d = jnp.sum(qs * _group_roll(kk, d) * w, axis=-1, keepdims=True)
            att = jnp.where(is_delta[d], a_d, att)
        pair_ref[h] = att

        for m, half in enumerate(halves):
            x = jnp.where(is_second[m], qs, kk) * jnp.exp2(-jnp.abs(cum - _half_end_rows(cum, half)))
            lvl_ref[m, rows] = x.astype(BF16)

        last = cum[n - 1:n]
        qdec_ref[rows] = (qs * jnp.exp2(cum)).astype(BF16)
        kdec_ref[rows] = (kk * jnp.exp2(last - cum)).astype(BF16)
        dec_ref[h] = jnp.exp2(last)

    for h in range(HEADS):
        rows = slice(h * n, (h + 1) * n)
        att = pair_ref[h]
        for m in range(len(halves)):
            x = lvl_ref[m, rows]
            a_m = lax.dot_general(x, x, (((1,), (1,)), ((), ())), preferred_element_type=F32)
            att = jnp.where(is_level[m], a_m, att)
        att_ref[h] = att.astype(BF16)

    for h in range(HEADS):
        hs = _head(h)
        rows = slice(h * n, (h + 1) * n)
        v_bf = v_ref[:, hs]
        st = st_ref[h]
        o = jnp.dot(att_ref[h], v_bf, preferred_element_type=F32)
        o = o + lax.dot_general(qdec_ref[rows], st.astype(BF16), (((1,), (1,)), ((), ())),
                                preferred_element_type=F32)
        st_ref[h] = st * dec_ref[h] + lax.dot_general(v_bf, kdec_ref[rows], (((0,), (0,)), ((), ())),
                                                      preferred_element_type=F32)
        ms = jnp.mean(o * o, axis=-1, keepdims=True)
        o_ref[:, hs] = (o * lax.rsqrt(ms + EPS) * ng_ref[...]).astype(o_ref.dtype)

    @pl.when(c == nc - 1)
    def _():
        for h in range(HEADS):
            sf_ref[0, h] = st_ref[h].T


def _hgrn(proj32, proj16, norm_g, state0, layer, *, nb, seq, row0, valid_len):
    n = HG_CHUNK
    nc = seq // n
    rb0 = row0 // n
    has_state = state0 is not None
    n_levels = (n // HG_SUB).bit_length() - 1

    def col(cidx):
        return pl.BlockSpec((n, WIDTH), lambda b, c: (rb0 + b * nc + c, cidx))

    in_specs = [col(P32_FC), col(P32_QC), col(P16_KC), col(P16_IC),
                pl.BlockSpec((1, HEAD_DIM), lambda b, c: (0, 0))]
    args = [proj32, proj32, proj16, proj16, norm_g.reshape(1, HEAD_DIM)]
    if has_state:
        in_specs.append(pl.BlockSpec((1, 1, HEADS, HEAD_DIM, HEAD_DIM), lambda b, c: (layer, b, 0, 0, 0)))
        args.append(state0)
    return pl.pallas_call(
        functools.partial(_hgrn_kernel, valid_len=valid_len, has_state=has_state),
        grid=(nb, nc),
        in_specs=in_specs,
        out_specs=[
            pl.BlockSpec((n, WIDTH), lambda b, c: (b * nc + c, 0)),
            pl.BlockSpec((1, HEADS, HEAD_DIM, HEAD_DIM), lambda b, c: (b, 0, 0, 0)),
        ],
        out_shape=[
            jax.ShapeDtypeStruct((nb * seq, WIDTH), BF16),
            jax.ShapeDtypeStruct((nb, HEADS, HEAD_DIM, HEAD_DIM), F32),
        ],
        scratch_shapes=[
            pltpu.VMEM((HEADS, HEAD_DIM, HEAD_DIM), F32),
            pltpu.VMEM((HEADS, n, n), F32),
            pltpu.VMEM((n_levels, HEADS * n, HEAD_DIM), BF16),
            pltpu.VMEM((HEADS, n, n), BF16),
            pltpu.VMEM((HEADS * n, HEAD_DIM), BF16),
            pltpu.VMEM((HEADS * n, HEAD_DIM), BF16),
            pltpu.VMEM((HEADS, 1, HEAD_DIM), F32),
        ],
        compiler_params=_params("parallel", "arbitrary"),
        name="hgrn",
    )(*args)


def _merge_kernel(h_ref, yap_ref, ybp_ref, ycp_ref, yas_ref, ybs_ref, ycs_ref, ga_ref, gb_ref, gc_ref,
                  ma_ref, mb_ref, mc_ref, wb_ref, wo_ref, *o_refs, prompt_tiles):
    is_prompt = pl.program_id(0) < prompt_tiles
    merged = None
    for n, (yp_ref, ys_ref, g_ref, m_ref) in enumerate(((yap_ref, yas_ref, ga_ref, ma_ref),
                                                        (ybp_ref, ybs_ref, gb_ref, mb_ref),
                                                        (ycp_ref, ycs_ref, gc_ref, mc_ref))):
        y = jnp.where(is_prompt, yp_ref[...], ys_ref[...]).astype(F32)
        y = (y * g_ref[...].astype(F32)).astype(BF16)
        p = m_ref[...].astype(F32) * jnp.dot(y, wb_ref[n], preferred_element_type=F32)
        merged = p if merged is None else merged + p
    out = h_ref[...] + jnp.dot(merged.astype(BF16), wo_ref[...], preferred_element_type=F32)
    if len(o_refs) == 1:
        o_refs[0][...] = out
    else:
        @pl.when(is_prompt)
        def _():
            o_refs[0][...] = out

        @pl.when(jnp.logical_not(is_prompt))
        def _():
            o_refs[1][...] = out


def _merge(h_all, y_prompt, y_sample, proj16, wb_bf, wo_bf, split):
    t = h_all.shape[0]
    tp, ts = y_prompt[0].shape[0], y_sample[0].shape[0]
    tm = next(c for c in (256, 128) if tp % c == 0 and ts % c == 0)
    assert tp + ts == t
    n_p = tp // tm
    row = pl.BlockSpec((tm, WIDTH), lambda i: (i, 0))
    prow = pl.BlockSpec((tm, WIDTH), lambda i: (jnp.minimum(i, n_p - 1), 0))
    srow = pl.BlockSpec((tm, WIDTH), lambda i: (jnp.maximum(i - n_p, 0), 0))

    def col(cidx):
        return pl.BlockSpec((tm, WIDTH), lambda i: (i, cidx))

    if split:
        out_specs = [prow, srow]
        out_shape = [jax.ShapeDtypeStruct((tp, D_MODEL), F32), jax.ShapeDtypeStruct((ts, D_MODEL), F32)]
    else:
        out_specs = row
        out_shape = jax.ShapeDtypeStruct((t, D_MODEL), F32)
    return pl.pallas_call(
        functools.partial(_merge_kernel, prompt_tiles=n_p),
        grid=(t // tm,),
        in_specs=[row, prow, prow, prow, srow, srow, srow,
                  col(P16_GA), col(P16_GB), col(P16_GC), col(P16_MA), col(P16_MB), col(P16_MC),
                  pl.BlockSpec((3, WIDTH, D_MODEL), lambda i: (0, 0, 0)),
                  pl.BlockSpec((D_MODEL, D_MODEL), lambda i: (0, 0))],
        out_specs=out_specs,
        out_shape=out_shape,
        compiler_params=_params("arbitrary"),
        name="merge",
    )(h_all, *y_prompt, *y_sample, *([proj16] * 6), wb_bf, wo_bf)


def kernel(x_prompt, x_sample, cache_k, cache_v, state_pool, state_hgrn, meta_tokens, norm_g, w_in,
           q_norm_g, k_norm_g, w_pool, pool_scale, hgrn_lower_bounds, hgrn_norm_g, w_branch, w_out):
    bp, seq, _ = x_prompt.shape
    bs, dseq, _ = x_sample.shape
    depth = w_in.shape[0]
    past = cache_k.shape[2]
    lp_real = N_META + seq
    lp = -(-lp_real // LANE) * LANE
    tp = bp * lp
    ts = bs * dseq

    meta = jnp.broadcast_to(meta_tokens.astype(F32)[None], (bp, N_META, D_MODEL))
    hp = jnp.concatenate([meta, x_prompt, jnp.zeros((bp, lp - lp_real, D_MODEL), F32)], axis=1)
    h_all = jnp.concatenate([hp.reshape(tp, D_MODEL), x_sample.reshape(ts, D_MODEL)], axis=0)

    cache_k2 = cache_k.reshape(depth, bs, past * HEADS, HEAD_DIM)
    cache_v2 = cache_v.reshape(depth, bs, past * HEADS, HEAD_DIM)
    prefix16 = jnp.pad(state_pool, ((0, 0), (0, 0), (1, 0), (0, 0)))

    kp, vp, pp, sp, ksm, vsm, psm, ssm = [], [], [], [], [], [], [], []
    for l in range(depth):
        proj32, proj16, keys, vals = _in_proj(h_all, norm_g[l], w_in[l].astype(BF16), q_norm_g[l], k_norm_g[l],
                                              hgrn_lower_bounds, l)
        w_pool_bf = w_pool[l].astype(BF16)

        ya_p = _pool(proj32, None, w_pool_bf, pool_scale[l], nb=bp, seq=lp, nseq=1, row0=0, pos0=0)
        yb_p = _attn_prompt(proj16, nb=bp, seq=lp)
        yc_p, st_p = _hgrn(proj32, proj16, hgrn_norm_g[l], None, l,
                           nb=bp, seq=lp, row0=0, valid_len=lp_real)
        ya_s = _pool(proj32, prefix16[l], w_pool_bf, pool_scale[l], nb=bs, seq=dseq,
                     nseq=math.gcd(bs, tp // dseq), row0=tp, pos0=past)
        yb_s = _attn_sample(proj16, cache_k2, cache_v2, l, nb=bs, seq=dseq, row0=tp)
        yc_s, st_s = _hgrn(proj32, proj16, hgrn_norm_g[l], state_hgrn, l,
                           nb=bs, seq=dseq, row0=tp, valid_len=None)

        h_all = _merge(h_all, (ya_p, yb_p, yc_p), (ya_s, yb_s, yc_s), proj16,
                       w_branch[l].astype(BF16), w_out[l].astype(BF16), split=(l == depth - 1))

        def pool_rows(r0, r1):
            return proj32[r0:r1, P32_UA * WIDTH:(P32_UA + 1) * WIDTH]

        def token_rows(a, r0, r1):
            return a[r0 * HEADS:r1 * HEADS].reshape(r1 - r0, HEADS, HEAD_DIM)

        kp += [token_rows(keys, b * lp, b * lp + lp_real) for b in range(bp)]
        vp += [token_rows(vals, b * lp, b * lp + lp_real) for b in range(bp)]
        pp.append(jnp.stack([pool_rows(b * lp + lp_real - (POOL_MAX - 1), b * lp + lp_real) for b in range(bp)]))
        sp.append(st_p)
        ksm.append(keys[tp * HEADS:].reshape(bs, dseq, HEADS, HEAD_DIM))
        vsm.append(vals[tp * HEADS:].reshape(bs, dseq, HEADS, HEAD_DIM))
        full_s = jnp.concatenate([state_pool[l], pool_rows(tp, tp + ts).reshape(bs, dseq, WIDTH)], axis=1)
        psm.append(full_s[:, -(POOL_MAX - 1):])
        ssm.append(st_s)

    h_prompt, h_sample = h_all
    y_prompt = h_prompt.reshape(bp, lp, D_MODEL)[:, N_META:lp_real]
    y_sample = h_sample.reshape(bs, dseq, D_MODEL)
    k_prompt = jnp.stack(kp).reshape(depth, bp, lp_real, HEADS, HEAD_DIM)
    v_prompt = jnp.stack(vp).reshape(depth, bp, lp_real, HEADS, HEAD_DIM)
    return (y_prompt, y_sample, k_prompt, v_prompt, jnp.stack(pp), jnp.stack(sp),
            jnp.stack(ksm), jnp.stack(vsm), jnp.stack(psm), jnp.stack(ssm))
```

```python
import functools
import math

import jax
import jax.numpy as jnp
from jax import lax
from jax.experimental import pallas as pl
from jax.experimental.pallas import tpu as pltpu

F32 = jnp.float32
BF16 = jnp.bfloat16

D_MODEL = 1024
WIDTH = 1024
N_SPLIT = 13
N_META = 16
POOL_WINDOWS = (2, 4, 8, 16)
POOL_GW = WIDTH // len(POOL_WINDOWS)
POOL_MAX = max(POOL_WINDOWS)
HEADS = 8
HEAD_DIM = WIDTH // HEADS
SB_SCALE = HEAD_DIM ** -0.5
LOG2E = 1.4426950408889634
EPS = 1e-6
LB_FLOOR = 1e-30
HG_CHUNK = 64
SUBLANES = 8
HG_SUB = 4
LANE = 128
TK_WIDE = 256
VMEM_LIMIT = 56 * 1024 * 1024

C_UA, C_GA, C_QB, C_KB, C_VB, C_GB, C_FC, C_QC, C_IC, C_GC, C_MA, C_MB, C_MC = range(N_SPLIT)


def _head(h):
    return slice(h * HEAD_DIM, (h + 1) * HEAD_DIM)


def _sigmoid(x):
    return 1.0 / (1.0 + jnp.exp(-x))


def _params(*sem):
    return pltpu.CompilerParams(dimension_semantics=sem, vmem_limit_bytes=VMEM_LIMIT)


STEP_GROUPS = (C_UA, C_KB, C_VB, C_FC, C_QC, C_QB, C_IC, C_GA, C_GB, C_GC, C_MA, C_MB, C_MC)
P32_UA, P32_FC, P32_QC = range(3)
N_P32 = 3
P16_KB, P16_VB, P16_KC, P16_QB, P16_IC, P16_GA, P16_GB, P16_GC, P16_MA, P16_MB, P16_MC = range(11)
N_P16 = 11
STEP_P32 = (0, 0, 0, 1, 2, 2, 2, 2, 2, 2, 2, 2, 2)
STEP_P16 = (0, 0, 1, 2, 2, 3, 4, 5, 6, 7, 8, 9, 10)
S_UA, S_KB, S_VB, S_FC, S_QC, S_QB, S_IC, S_GA = range(8)
S_MA = S_GA + 3


def _in_proj_kernel(tab_ref, x_ref, g_ref, w_ref, qg_ref, kg_ref, lbp_ref, o32_ref, o16_ref, kh_ref, vh_ref,
                    xn_ref, *, layer):
    j = pl.program_id(1)
    tm = x_ref.shape[0]

    @pl.when(j == 0)
    def _():
        x = x_ref[...]
        ms = jnp.mean(x * x, axis=-1, keepdims=True)
        xn_ref[...] = (x * lax.rsqrt(ms + EPS) * g_ref[...]).astype(BF16)

    def project():
        return jnp.dot(xn_ref[...], w_ref[...], preferred_element_type=F32)

    def head_norm(a, gain):
        ms = jnp.mean(a * a, axis=-1, keepdims=True)
        return a * lax.rsqrt(ms + EPS) * gain

    def head_rows(h):
        return pl.ds(h, tm, stride=HEADS)

    @pl.when(j == S_UA)
    def _():
        o32_ref[...] = project()

    @pl.when(j == S_FC)
    def _():
        lbp = lbp_ref[...]
        e = jnp.exp(lbp - jnp.max(lbp, axis=0, keepdims=True))
        soft = e / jnp.sum(e, axis=0, keepdims=True)
        run = soft[0:1]
        for l in range(1, layer + 1):
            run = run + soft[l:l + 1]
        lb = run - soft[0:1]
        z = project()
        ez = jnp.exp(-jnp.abs(z))
        r = 1.0 / (1.0 + ez)
        er = ez * r
        pos = z >= 0.0
        gate = 1.0 - lb
        f = jnp.maximum(lb, LB_FLOOR) + gate * jnp.where(pos, r, er)
        o32_ref[...] = jnp.log2(f)
        o16_ref[...] = (gate * jnp.where(pos, er, r)).astype(BF16)

    @pl.when(j == S_QC)
    def _():
        q = project()
        o32_ref[...] = q * _sigmoid(q)

    @pl.when(j == S_KB)
    def _():
        acc = project()
        for h in range(HEADS):
            kn = head_norm(acc[:, _head(h)], kg_ref[...])
            kh_ref[head_rows(h), :] = kn
            o16_ref[:, _head(h)] = kn.astype(BF16)

    @pl.when(j == S_VB)
    def _():
        acc = project()
        for h in range(HEADS):
            vh_ref[head_rows(h), :] = acc[:, _head(h)]
        o16_ref[...] = acc.astype(BF16)

    @pl.when(j == S_QB)
    def _():
        acc = project()
        for h in range(HEADS):
            qn = head_norm(acc[:, _head(h)], qg_ref[...])
            o16_ref[:, _head(h)] = (qn * (SB_SCALE * LOG2E)).astype(BF16)

    @pl.when(j == S_IC)
    def _():
        o16_ref[...] = project().astype(BF16)

    @pl.when(jnp.logical_and(j >= S_GA, j < S_MA))
    def _():
        g = project()
        o16_ref[...] = (g * _sigmoid(g)).astype(BF16)

    @pl.when(j >= S_MA)
    def _():
        o16_ref[...] = _sigmoid(project()).astype(BF16)


def _pick_tile(t, candidates):
    return next(c for c in candidates if t % c == 0)


def _in_proj(h_all, norm_g, w_bf, q_g, k_g, lb_param, layer):
    t = h_all.shape[0]
    tm = _pick_tile(t, (1024, 512, 256, 128))
    depth = lb_param.shape[0]
    table = jnp.array([STEP_GROUPS, STEP_P32, STEP_P16], jnp.int32)
    grid_spec = pltpu.PrefetchScalarGridSpec(
        num_scalar_prefetch=1,
        grid=(t // tm, N_SPLIT),
        in_specs=[
            pl.BlockSpec((tm, D_MODEL), lambda i, j, tab: (i, 0)),
            pl.BlockSpec((1, D_MODEL), lambda i, j, tab: (0, 0)),
            pl.BlockSpec((D_MODEL, WIDTH), lambda i, j, tab: (0, tab[0, j])),
            pl.BlockSpec((1, HEAD_DIM), lambda i, j, tab: (0, 0)),
            pl.BlockSpec((1, HEAD_DIM), lambda i, j, tab: (0, 0)),
            pl.BlockSpec((depth, WIDTH), lambda i, j, tab: (0, 0)),
        ],
        out_specs=[
            pl.BlockSpec((tm, WIDTH), lambda i, j, tab: (i, tab[1, j])),
            pl.BlockSpec((tm, WIDTH), lambda i, j, tab: (i, tab[2, j])),
            pl.BlockSpec((tm * HEADS, HEAD_DIM), lambda i, j, tab: (i, 0)),
            pl.BlockSpec((tm * HEADS, HEAD_DIM), lambda i, j, tab: (i, 0)),
        ],
        scratch_shapes=[pltpu.VMEM((tm, D_MODEL), BF16)],
    )
    return pl.pallas_call(
        functools.partial(_in_proj_kernel, layer=layer),
        grid_spec=grid_spec,
        out_shape=[jax.ShapeDtypeStruct((t, N_P32 * WIDTH), F32),
                   jax.ShapeDtypeStruct((t, N_P16 * WIDTH), BF16),
                   jax.ShapeDtypeStruct((t * HEADS, HEAD_DIM), F32),
                   jax.ShapeDtypeStruct((t * HEADS, HEAD_DIM), F32)],
        compiler_params=_params("arbitrary", "arbitrary"),
        name="in_proj",
    )(table, h_all, norm_g.reshape(1, D_MODEL), w_bf, q_g.reshape(1, HEAD_DIM), k_g.reshape(1, HEAD_DIM),
      lb_param)


def _pool_kernel(*refs, nseq, seq, pos0, has_prefix):
    if has_prefix:
        u_ref, p_ref, w_ref, s_ref, o_ref = refs
    else:
        u_ref, w_ref, s_ref, o_ref = refs
    g = pl.program_id(1)
    u = u_ref[...].reshape(nseq, seq, POOL_GW)
    head = p_ref[...] if has_prefix else jnp.zeros((nseq, POOL_MAX, POOL_GW), F32)
    x = jnp.concatenate([head, u], axis=1)
    pos = pos0 + lax.broadcasted_iota(jnp.int32, (1, seq, 1), 1)

    for gi, win in enumerate(POOL_WINDOWS):
        @pl.when(g == gi)
        def _(win=win):
            s = x
            sh = 1
            while sh < win:
                s = s + pltpu.roll(s, sh, 1)
                sh *= 2
            cnt = jnp.minimum(pos + 1, win).astype(F32)
            diff = (s[:, POOL_MAX:] / cnt - u).reshape(nseq * seq, POOL_GW)
            y = jnp.dot(diff.astype(BF16), w_ref[0], preferred_element_type=F32)
            o_ref[...] = (y * s_ref[...]).astype(o_ref.dtype)


def _pool(proj32, prefix16, w_pool_bf, scale, *, nb, seq, nseq, row0, pos0):
    rows = nseq * seq
    rb0 = row0 // rows
    has_prefix = prefix16 is not None
    ua0 = P32_UA * len(POOL_WINDOWS)
    in_specs = [pl.BlockSpec((rows, POOL_GW), lambda b, g: (rb0 + b, ua0 + g))]
    args = [proj32]
    if has_prefix:
        in_specs.append(pl.BlockSpec((nseq, POOL_MAX, POOL_GW), lambda b, g: (b, 0, g)))
        args.append(prefix16)
    in_specs += [
        pl.BlockSpec((1, POOL_GW, POOL_GW), lambda b, g: (g, 0, 0)),
        pl.BlockSpec((1, POOL_GW), lambda b, g: (0, g)),
    ]
    args += [w_pool_bf, scale.reshape(1, WIDTH)]
    return pl.pallas_call(
        functools.partial(_pool_kernel, nseq=nseq, seq=seq, pos0=pos0, has_prefix=has_prefix),
        grid=(nb // nseq, len(POOL_WINDOWS)),
        in_specs=in_specs,
        out_specs=pl.BlockSpec((rows, POOL_GW), lambda b, g: (b, g)),
        out_shape=jax.ShapeDtypeStruct((nb * seq, WIDTH), BF16),
        compiler_params=_params("parallel", "parallel"),
        name="pool",
    )(*args)


def _tri_after(n):
    r = lax.broadcasted_iota(jnp.int32, (n, n), 0)
    c = lax.broadcasted_iota(jnp.int32, (n, n), 1)
    return jnp.where(r > c, 1.0, 0.0).astype(BF16)


def _causal_mask(tq, tk, shift):
    r = lax.broadcasted_iota(jnp.int32, (tq, tk), 0)
    c = lax.broadcasted_iota(jnp.int32, (tq, tk), 1)
    return c < r + shift


def _sb_heads(q_ref, key, val, tk, tri, carry_ref, keep_ref, logit_ref, o_ref, mask=None):
    tq = q_ref.shape[0]
    first = mask is not None
    for h in range(HEADS):
        rows = slice(h * tq, (h + 1) * tq)
        z = lax.dot_general(q_ref[:, _head(h)], key(h), (((1,), (1,)), ((), ())),
                            preferred_element_type=F32)
        soft = jnp.maximum(z, 0.0) + jnp.log2(1.0 + jnp.exp2(-jnp.abs(z)))
        if first:
            logit_ref[rows, :tk] = z - soft
            soft = jnp.where(mask, soft, 0.0)
            carry_ref[h] = jnp.sum(soft, axis=1, keepdims=True)
        else:
            carry = carry_ref[h]
            logit_ref[rows, :tk] = z - soft - carry
            carry_ref[h] = carry + jnp.sum(soft, axis=1, keepdims=True)
        keep_ref[rows, :tk] = soft.astype(BF16)
    later = jnp.dot(keep_ref[:, :tk], tri, preferred_element_type=F32)
    for h in range(HEADS):
        rows = slice(h * tq, (h + 1) * tq)
        wts = jnp.exp2(logit_ref[rows, :tk] - later[rows])
        if first:
            wts = jnp.where(mask, wts, 0.0)
        pv = jnp.dot(wts.astype(BF16), val(h), preferred_element_type=F32)
        if first:
            o_ref[:, _head(h)] = pv
        else:
            o_ref[:, _head(h)] += pv


def _attn_scratch(tq):
    return [pltpu.VMEM((tq, WIDTH), F32),
            pltpu.VMEM((TK_WIDE, TK_WIDE), BF16),
            pltpu.VMEM((HEADS, tq, 1), F32),
            pltpu.VMEM((HEADS * tq, TK_WIDE), BF16),
            pltpu.VMEM((HEADS * tq, TK_WIDE), F32)]


def _attn_prompt_kernel(q_ref, k_ref, v_ref, o_ref, acc_ref, triw_ref, carry_ref, keep_ref, logit_ref):
    i = pl.program_id(1)
    tq = q_ref.shape[0]
    triw_ref[...] = _tri_after(TK_WIDE)
    d0 = pl.multiple_of(i * tq, tq)
    odd = jnp.bitwise_and(i, 1)

    def sweep(k0, tk, tri, mask=None):
        _sb_heads(q_ref, lambda h: k_ref[pl.ds(k0, tk), _head(h)], lambda h: v_ref[pl.ds(k0, tk), _head(h)],
                  tk, tri, carry_ref, keep_ref, logit_ref, acc_ref, mask)

    @pl.when(odd == 0)
    def _():
        sweep(d0, tq, _tri_after(tq), _causal_mask(tq, tq, 0))

    @pl.when(odd == 1)
    def _():
        sweep(pl.multiple_of(d0 - tq, tq), TK_WIDE, triw_ref[...], _causal_mask(tq, TK_WIDE, tq))

    first_wide = d0 - odd * tq

    @pl.loop(0, lax.shift_right_logical(i, 1))
    def _(t):
        sweep(pl.multiple_of(first_wide - (t + 1) * TK_WIDE, tq), TK_WIDE, triw_ref[...])

    o_ref[...] = acc_ref[...].astype(o_ref.dtype)


def _attn_prompt(proj16, *, nb, seq):
    nq = seq // LANE
    return pl.pallas_call(
        _attn_prompt_kernel,
        grid=(nb, nq),
        in_specs=[
            pl.BlockSpec((LANE, WIDTH), lambda b, i: (b * nq + i, P16_QB)),
            pl.BlockSpec((seq, WIDTH), lambda b, i: (b, P16_KB)),
            pl.BlockSpec((seq, WIDTH), lambda b, i: (b, P16_VB)),
        ],
        out_specs=pl.BlockSpec((LANE, WIDTH), lambda b, i: (b * nq + i, 0)),
        out_shape=jax.ShapeDtypeStruct((nb * seq, WIDTH), BF16),
        scratch_shapes=_attn_scratch(LANE),
        compiler_params=_params("parallel", "arbitrary"),
        name="attn_prompt",
    )(proj16, proj16, proj16)


def _attn_sample_kernel(q_ref, kn_ref, vn_ref, kp_ref, vp_ref, o_ref, acc_ref, triw_ref, carry_ref, keep_ref,
                        logit_ref, *, past):
    tq = q_ref.shape[0]
    triw_ref[...] = _tri_after(TK_WIDE)
    n_past = past // TK_WIDE

    _sb_heads(q_ref, lambda h: kn_ref[:, _head(h)], lambda h: vn_ref[:, _head(h)],
              tq, _tri_after(tq), carry_ref, keep_ref, logit_ref, acc_ref, _causal_mask(tq, tq, 0))

    @pl.loop(0, n_past)
    def _(t):
        k0 = (n_past - 1 - t) * TK_WIDE

        def head_rows(ref, h):
            rows = pl.ds(pl.multiple_of(k0 * HEADS, HEADS) + h, TK_WIDE, stride=HEADS)
            return ref[0, 0, rows, :].astype(BF16)

        _sb_heads(q_ref, lambda h: head_rows(kp_ref, h), lambda h: head_rows(vp_ref, h),
                  TK_WIDE, triw_ref[...], carry_ref, keep_ref, logit_ref, acc_ref)

    o_ref[...] = acc_ref[...].astype(o_ref.dtype)


def _attn_sample(proj16, cache_k, cache_v, layer, *, nb, seq, row0):
    rows = cache_k.shape[2]
    rb0 = row0 // seq
    return pl.pallas_call(
        functools.partial(_attn_sample_kernel, past=rows // HEADS),
        grid=(nb,),
        in_specs=[
            pl.BlockSpec((seq, WIDTH), lambda b: (rb0 + b, P16_QB)),
            pl.BlockSpec((seq, WIDTH), lambda b: (rb0 + b, P16_KB)),
            pl.BlockSpec((seq, WIDTH), lambda b: (rb0 + b, P16_VB)),
            pl.BlockSpec((1, 1, rows, HEAD_DIM), lambda b: (layer, b, 0, 0)),
            pl.BlockSpec((1, 1, rows, HEAD_DIM), lambda b: (layer, b, 0, 0)),
        ],
        out_specs=pl.BlockSpec((seq, WIDTH), lambda b: (b, 0)),
        out_shape=jax.ShapeDtypeStruct((nb * seq, WIDTH), BF16),
        scratch_shapes=_attn_scratch(seq),
        compiler_params=_params("parallel"),
        name="attn_sample",
    )(proj16, proj16, proj16, cache_k, cache_v)


def _group_roll(x, d):
    n, lanes = x.shape
    return pltpu.roll(x.reshape(n // SUBLANES, SUBLANES, lanes), d, 1).reshape(n, lanes)


def _cumsum_rows(x, sub):
    n = x.shape[0]
    sh = 1
    while sh < SUBLANES:
        x = x + jnp.where(sub >= sh, _group_roll(x, sh), 0.0)
        sh *= 2
    parts, run = [], None
    for g in range(n // SUBLANES):
        blk = x[g * SUBLANES:(g + 1) * SUBLANES]
        if run is not None:
            blk = blk + run
        parts.append(blk)
        run = blk[SUBLANES - 1:SUBLANES]
    return jnp.concatenate(parts, axis=0)


def _half_end_rows(x, half):
    n, lanes = x.shape
    blk = 2 * half
    if blk <= SUBLANES:
        x3 = x.reshape(n // SUBLANES, SUBLANES, lanes)
        parts = [jnp.broadcast_to(x3[:, s + half - 1:s + half, :], (n // SUBLANES, blk, lanes))
                 for s in range(0, SUBLANES, blk)]
        return jnp.concatenate(parts, axis=1).reshape(n, lanes)
    return jnp.concatenate([jnp.broadcast_to(x[s + half - 1:s + half], (blk, lanes))
                            for s in range(0, n, blk)], axis=0)


def _pair_classes(n):
    r = lax.broadcasted_iota(jnp.int32, (n, n), 0)
    c = lax.broadcasted_iota(jnp.int32, (n, n), 1)

    def block_of(x, size):
        return jnp.bitwise_and(x, -size)

    same = block_of(r, HG_SUB) == block_of(c, HG_SUB)
    delta = jnp.where(jnp.logical_and(same, c <= r), r - c, -1)
    level = jnp.full((n, n), -1, jnp.int32)
    half, m = HG_SUB, 0
    while half < n:
        blk = 2 * half
        hit = jnp.logical_and(block_of(r, blk) == block_of(c, blk),
                              jnp.logical_and(jnp.bitwise_and(r, half) != 0, jnp.bitwise_and(c, half) == 0))
        level = jnp.where(hit, m, level)
        half, m = blk, m + 1
    return delta, level


def _hgrn_kernel(*refs, valid_len, has_state):
    refs = list(refs)
    f_ref, q_ref, k_ref, v_ref, ng_ref = refs[:5]
    s0_ref = refs[5] if has_state else None
    o_ref, sf_ref, st_ref, pair_ref, lvl_ref, att_ref, qdec_ref, kdec_ref, dec_ref = refs[-9:]
    c = pl.program_id(1)
    nc = pl.num_programs(1)
    n = HG_CHUNK

    row = lax.broadcasted_iota(jnp.int32, (n, 1), 0)
    valid = None if valid_len is None else (c * n + row) < valid_len
    sub = jnp.bitwise_and(row, SUBLANES - 1)
    delta, level = _pair_classes(n)
    is_delta = [delta == d for d in range(HG_SUB)]
    halves = []
    half = HG_SUB
    while half < n:
        halves.append(half)
        half *= 2
    is_level = [level == m for m in range(len(halves))]
    is_second = [jnp.bitwise_and(row, half) != 0 for half in halves]

    @pl.when(c == 0)
    def _():
        for h in range(HEADS):
            st_ref[h] = s0_ref[0, 0, h].T if has_state else jnp.zeros((HEAD_DIM, HEAD_DIM), F32)

    for h in range(HEADS):
        hs = _head(h)
        rows = slice(h * n, (h + 1) * n)
        log_f = f_ref[:, hs]
        kk = k_ref[:, hs].astype(F32)
        qs = q_ref[:, hs]
        if valid_len is not None:
            log_f = jnp.where(valid, log_f, 0.0)
            kk = jnp.where(valid, kk, 0.0)
        cum = _cumsum_rows(log_f, sub)

        att = jnp.where(is_delta[0], jnp.sum(qs * kk, axis=-1, keepdims=True), 0.0)
        for d in range(1, HG_SUB):
            w = jnp.exp2(jnp.minimum(cum - _group_roll(cum, d), 0.0))
            a_d = jnp.sum(qs * _group_roll(kk, d) * w, axis=-1, keepdims=True)
            att = jnp.where(is_delta[d], a_d, att)
        pair_ref[h] = att

        for m, half in enumerate(halves):
            x = jnp.where(is_second[m], qs, kk) * jnp.exp2(-jnp.abs(cum - _half_end_rows(cum, half)))
            lvl_ref[m, rows] = x.astype(BF16)

        last = cum[n - 1:n]
        qdec_ref[rows] = (qs * jnp.exp2(cum)).astype(BF16)
        kdec_ref[rows] = (kk * jnp.exp2(last - cum)).astype(BF16)
        dec_ref[h] = jnp.exp2(last)

    for h in range(HEADS):
        rows = slice(h * n, (h + 1) * n)
        att = pair_ref[h]
        for m in range(len(halves)):
            x = lvl_ref[m, rows]
            a_m = lax.dot_general(x, x, (((1,), (1,)), ((), ())), preferred_element_type=F32)
            att = jnp.where(is_level[m], a_m, att)
        att_ref[h] = att.astype(BF16)

    for h in range(HEADS):
        hs = _head(h)
        rows = slice(h * n, (h + 1) * n)
        v_bf = v_ref[:, hs]
        st = st_ref[h]
        o = jnp.dot(att_ref[h], v_bf, preferred_element_type=F32)
        o = o + lax.dot_general(qdec_ref[rows], st.astype(BF16), (((1,), (1,)), ((), ())),
                                preferred_element_type=F32)
        st_ref[h] = st * dec_ref[h] + lax.dot_general(v_bf, kdec_ref[rows], (((0,), (0,)), ((), ())),
                                                      preferred_element_type=F32)
        ms = jnp.mean(o * o, axis=-1, keepdims=True)
        o_ref[:, hs] = (o * lax.rsqrt(ms + EPS) * ng_ref[...]).astype(o_ref.dtype)

    @pl.when(c == nc - 1)
    def _():
        for h in range(HEADS):
            sf_ref[0, h] = st_ref[h].T


def _hgrn(proj32, proj16, norm_g, state0, layer, *, nb, seq, row0, valid_len):
    n = HG_CHUNK
    nc = seq // n
    rb0 = row0 // n
    has_state = state0 is not None
    n_levels = (n // HG_SUB).bit_length() - 1

    def col(cidx):
        return pl.BlockSpec((n, WIDTH), lambda b, c: (rb0 + b * nc + c, cidx))

    in_specs = [col(P32_FC), col(P32_QC), col(P16_KC), col(P16_IC),
                pl.BlockSpec((1, HEAD_DIM), lambda b, c: (0, 0))]
    args = [proj32, proj32, proj16, proj16, norm_g.reshape(1, HEAD_DIM)]
    if has_state:
        in_specs.append(pl.BlockSpec((1, 1, HEADS, HEAD_DIM, HEAD_DIM), lambda b, c: (layer, b, 0, 0, 0)))
        args.append(state0)
    return pl.pallas_call(
        functools.partial(_hgrn_kernel, valid_len=valid_len, has_state=has_state),
        grid=(nb, nc),
        in_specs=in_specs,
        out_specs=[
            pl.BlockSpec((n, WIDTH), lambda b, c: (b * nc + c, 0)),
            pl.BlockSpec((1, HEADS, HEAD_DIM, HEAD_DIM), lambda b, c: (b, 0, 0, 0)),
        ],
        out_shape=[
            jax.ShapeDtypeStruct((nb * seq, WIDTH), BF16),
            jax.ShapeDtypeStruct((nb, HEADS, HEAD_DIM, HEAD_DIM), F32),
        ],
        scratch_shapes=[
            pltpu.VMEM((HEADS, HEAD_DIM, HEAD_DIM), F32),
            pltpu.VMEM((HEADS, n, n), F32),
            pltpu.VMEM((n_levels, HEADS * n, HEAD_DIM), BF16),
            pltpu.VMEM((HEADS, n, n), BF16),
            pltpu.VMEM((HEADS * n, HEAD_DIM), BF16),
            pltpu.VMEM((HEADS * n, HEAD_DIM), BF16),
            pltpu.VMEM((HEADS, 1, HEAD_DIM), F32),
        ],
        compiler_params=_params("parallel", "arbitrary"),
        name="hgrn",
    )(*args)


def _merge_kernel(h_ref, yap_ref, ybp_ref, ycp_ref, yas_ref, ybs_ref, ycs_ref, ga_ref, gb_ref, gc_ref,
                  ma_ref, mb_ref, mc_ref, wb_ref, wo_ref, *o_refs, prompt_tiles):
    is_prompt = pl.program_id(0) < prompt_tiles
    merged = None
    for n, (yp_ref, ys_ref, g_ref, m_ref) in enumerate(((yap_ref, yas_ref, ga_ref, ma_ref),
                                                        (ybp_ref, ybs_ref, gb_ref, mb_ref),
                                                        (ycp_ref, ycs_ref, gc_ref, mc_ref))):
        y = jnp.where(is_prompt, yp_ref[...], ys_ref[...]).astype(F32)
        y = (y * g_ref[...].astype(F32)).astype(BF16)
        p = m_ref[...].astype(F32) * jnp.dot(y, wb_ref[n], preferred_element_type=F32)
        merged = p if merged is None else merged + p
    out = h_ref[...] + jnp.dot(merged.astype(BF16), wo_ref[...], preferred_element_type=F32)
    if len(o_refs) == 1:
        o_refs[0][...] = out
    else:
        @pl.when(is_prompt)
        def _():
            o_refs[0][...] = out

        @pl.when(jnp.logical_not(is_prompt))
        def _():
            o_refs[1][...] = out


def _merge(h_all, y_prompt, y_sample, proj16, wb_bf, wo_bf, split):
    t = h_all.shape[0]
    tp, ts = y_prompt[0].shape[0], y_sample[0].shape[0]
    tm = next(c for c in (256, 128) if tp % c == 0 and ts % c == 0)
    assert tp + ts == t
    n_p = tp // tm
    row = pl.BlockSpec((tm, WIDTH), lambda i: (i, 0))
    prow = pl.BlockSpec((tm, WIDTH), lambda i: (jnp.minimum(i, n_p - 1), 0))
    srow = pl.BlockSpec((tm, WIDTH), lambda i: (jnp.maximum(i - n_p, 0), 0))

    def col(cidx):
        return pl.BlockSpec((tm, WIDTH), lambda i: (i, cidx))

    if split:
        out_specs = [prow, srow]
        out_shape = [jax.ShapeDtypeStruct((tp, D_MODEL), F32), jax.ShapeDtypeStruct((ts, D_MODEL), F32)]
    else:
        out_specs = row
        out_shape = jax.ShapeDtypeStruct((t, D_MODEL), F32)
    return pl.pallas_call(
        functools.partial(_merge_kernel, prompt_tiles=n_p),
        grid=(t // tm,),
        in_specs=[row, prow, prow, prow, srow, srow, srow,
                  col(P16_GA), col(P16_GB), col(P16_GC), col(P16_MA), col(P16_MB), col(P16_MC),
                  pl.BlockSpec((3, WIDTH, D_MODEL), lambda i: (0, 0, 0)),
                  pl.BlockSpec((D_MODEL, D_MODEL), lambda i: (0, 0))],
        out_specs=out_specs,
        out_shape=out_shape,
        compiler_params=_params("arbitrary"),
        name="merge",
    )(h_all, *y_prompt, *y_sample, *([proj16] * 6), wb_bf, wo_bf)


def kernel(x_prompt, x_sample, cache_k, cache_v, state_pool, state_hgrn, meta_tokens, norm_g, w_in,
           q_norm_g, k_norm_g, w_pool, pool_scale, hgrn_lower_bounds, hgrn_norm_g, w_branch, w_out):
    bp, seq, _ = x_prompt.shape
    bs, dseq, _ = x_sample.shape
    depth = w_in.shape[0]
    past = cache_k.shape[2]
    lp_real = N_META + seq
    lp = -(-lp_real // LANE) * LANE
    tp = bp * lp
    ts = bs * dseq

    pad_rows = jnp.zeros((lp - lp_real, D_MODEL), F32)
    pieces = []
    for b in range(bp):
        pieces += [meta_tokens.astype(F32), x_prompt[b], pad_rows]
    h_all = jnp.concatenate(pieces + [x_sample.reshape(ts, D_MODEL)], axis=0)

    cache_k2 = cache_k.reshape(depth, bs, past * HEADS, HEAD_DIM)
    cache_v2 = cache_v.reshape(depth, bs, past * HEADS, HEAD_DIM)
    prefix16 = jnp.pad(state_pool, ((0, 0), (0, 0), (1, 0), (0, 0)))

    kp, vp, pp, sp, ksm, vsm, psm, ssm = [], [], [], [], [], [], [], []
    for l in range(depth):
        proj32, proj16, keys, vals = _in_proj(h_all, norm_g[l], w_in[l].astype(BF16), q_norm_g[l], k_norm_g[l],
                                              hgrn_lower_bounds, l)
        w_pool_bf = w_pool[l].astype(BF16)

        ya_p = _pool(proj32, None, w_pool_bf, pool_scale[l], nb=bp, seq=lp, nseq=1, row0=0, pos0=0)
        yb_p = _attn_prompt(proj16, nb=bp, seq=lp)
        yc_p, st_p = _hgrn(proj32, proj16, hgrn_norm_g[l], None, l,
                           nb=bp, seq=lp, row0=0, valid_len=lp_real)
        ya_s = _pool(proj32, prefix16[l], w_pool_bf, pool_scale[l], nb=bs, seq=dseq,
                     nseq=math.gcd(bs, tp // dseq), row0=tp, pos0=past)
        yb_s = _attn_sample(proj16, cache_k2, cache_v2, l, nb=bs, seq=dseq, row0=tp)
        yc_s, st_s = _hgrn(proj32, proj16, hgrn_norm_g[l], state_hgrn, l,
                           nb=bs, seq=dseq, row0=tp, valid_len=None)

        h_all = _merge(h_all, (ya_p, yb_p, yc_p), (ya_s, yb_s, yc_s), proj16,
                       w_branch[l].astype(BF16), w_out[l].astype(BF16), split=(l == depth - 1))

        def pool_rows(r0, r1):
            return proj32[r0:r1, P32_UA * WIDTH:(P32_UA + 1) * WIDTH]

        def token_rows(a, r0, r1):
            return a[r0 * HEADS:r1 * HEADS].reshape(r1 - r0, HEADS, HEAD_DIM)

        kp += [token_rows(keys, b * lp, b * lp + lp_real) for b in range(bp)]
        vp += [token_rows(vals, b * lp, b * lp + lp_real) for b in range(bp)]
        pp.append(jnp.stack([pool_rows(b * lp + lp_real - (POOL_MAX - 1), b * lp + lp_real) for b in range(bp)]))
        sp.append(st_p)
        ksm.append(keys[tp * HEADS:].reshape(bs, dseq, HEADS, HEAD_DIM))
        vsm.append(vals[tp * HEADS:].reshape(bs, dseq, HEADS, HEAD_DIM))
        full_s = jnp.concatenate([state_pool[l], pool_rows(tp, tp + ts).reshape(bs, dseq, WIDTH)], axis=1)
        psm.append(full_s[:, -(POOL_MAX - 1):])
        ssm.append(st_s)

    h_prompt, h_sample = h_all
    y_prompt = h_prompt.reshape(bp, lp, D_MODEL)[:, N_META:lp_real]
    y_sample = h_sample.reshape(bs, dseq, D_MODEL)
    k_prompt = jnp.stack(kp).reshape(depth, bp, lp_real, HEADS, HEAD_DIM)
    v_prompt = jnp.stack(vp).reshape(depth, bp, lp_real, HEADS, HEAD_DIM)
    return (y_prompt, y_sample, k_prompt, v_prompt, jnp.stack(pp), jnp.stack(sp),
            jnp.stack(ksm), jnp.stack(vsm), jnp.stack(psm), jnp.stack(ssm))
```

```python
import functools
import math

import jax
import jax.numpy as jnp
from jax import lax
from jax.experimental import pallas as pl
from jax.experimental.pallas import tpu as pltpu

F32 = jnp.float32
BF16 = jnp.bfloat16

D_MODEL = 1024
WIDTH = 1024
N_SPLIT = 13
N_META = 16
POOL_WINDOWS = (2, 4, 8, 16)
POOL_GW = WIDTH // len(POOL_WINDOWS)
POOL_MAX = max(POOL_WINDOWS)
HEADS = 8
HEAD_DIM = WIDTH // HEADS
SB_SCALE = HEAD_DIM ** -0.5
LOG2E = 1.4426950408889634
EPS = 1e-6
LB_FLOOR = 1e-30
HG_CHUNK = 64
SUBLANES = 8
HG_SUB = 4
LANE = 128
TK_WIDE = 256
VMEM_LIMIT = 56 * 1024 * 1024

C_UA, C_GA, C_QB, C_KB, C_VB, C_GB, C_FC, C_QC, C_IC, C_GC, C_MA, C_MB, C_MC = range(N_SPLIT)


def _head(h):
    return slice(h * HEAD_DIM, (h + 1) * HEAD_DIM)


def _sigmoid(x):
    return 0.5 * jnp.tanh(0.5 * x) + 0.5


def _params(*sem):
    return pltpu.CompilerParams(dimension_semantics=sem, vmem_limit_bytes=VMEM_LIMIT)


STEP_GROUPS = (C_UA, C_KB, C_VB, C_FC, C_QC, C_QB, C_IC, C_GA, C_GB, C_GC, C_MA, C_MB, C_MC)
P32_UA, P32_FC, P32_QC = range(3)
N_P32 = 3
P16_KB, P16_VB, P16_KC, P16_QB, P16_IC, P16_GA, P16_GB, P16_GC, P16_MA, P16_MB, P16_MC = range(11)
N_P16 = 11
STEP_P32 = (0, 0, 0, 1, 2, 2, 2, 2, 2, 2, 2, 2, 2)
STEP_P16 = (0, 0, 1, 2, 2, 3, 4, 5, 6, 7, 8, 9, 10)
S_UA, S_KB, S_VB, S_FC, S_QC, S_QB, S_IC, S_GA = range(8)
S_MA = S_GA + 3


def _in_proj_kernel(tab_ref, x_ref, g_ref, w_ref, qg_ref, kg_ref, lbp_ref, o32_ref, o16_ref, kh_ref, vh_ref,
                    xn_ref, *, layer):
    j = pl.program_id(1)
    tm = x_ref.shape[0]

    @pl.when(j == 0)
    def _():
        x = x_ref[...]
        ms = jnp.mean(x * x, axis=-1, keepdims=True)
        xn_ref[...] = (x * lax.rsqrt(ms + EPS) * g_ref[...]).astype(BF16)

    def project():
        return jnp.dot(xn_ref[...], w_ref[...], preferred_element_type=F32)

    def head_norm(a, gain):
        ms = jnp.mean(a * a, axis=-1, keepdims=True)
        return a * lax.rsqrt(ms + EPS) * gain

    def head_rows(h):
        return pl.ds(h, tm, stride=HEADS)

    @pl.when(j == S_UA)
    def _():
        o32_ref[...] = project()

    @pl.when(j == S_FC)
    def _():
        lbp = lbp_ref[...]
        e = jnp.exp(lbp - jnp.max(lbp, axis=0, keepdims=True))
        soft = e / jnp.sum(e, axis=0, keepdims=True)
        run = soft[0:1]
        for l in range(1, layer + 1):
            run = run + soft[l:l + 1]
        lb = run - soft[0:1]
        z = project()
        ez = jnp.exp(-jnp.abs(z))
        r = 1.0 / (1.0 + ez)
        er = ez * r
        pos = z >= 0.0
        gate = 1.0 - lb
        f = jnp.maximum(lb, LB_FLOOR) + gate * jnp.where(pos, r, er)
        o32_ref[...] = jnp.log2(f)
        o16_ref[...] = (gate * jnp.where(pos, er, r)).astype(BF16)

    @pl.when(j == S_QC)
    def _():
        q = project()
        o32_ref[...] = q * _sigmoid(q)

    @pl.when(j == S_KB)
    def _():
        acc = project()
        for h in range(HEADS):
            kn = head_norm(acc[:, _head(h)], kg_ref[...])
            kh_ref[head_rows(h), :] = kn
            o16_ref[:, _head(h)] = kn.astype(BF16)

    @pl.when(j == S_VB)
    def _():
        acc = project()
        for h in range(HEADS):
            vh_ref[head_rows(h), :] = acc[:, _head(h)]
        o16_ref[...] = acc.astype(BF16)

    @pl.when(j == S_QB)
    def _():
        acc = project()
        for h in range(HEADS):
            qn = head_norm(acc[:, _head(h)], qg_ref[...])
            o16_ref[:, _head(h)] = (qn * (SB_SCALE * LOG2E)).astype(BF16)

    @pl.when(j == S_IC)
    def _():
        o16_ref[...] = project().astype(BF16)

    @pl.when(jnp.logical_and(j >= S_GA, j < S_MA))
    def _():
        g = project()
        o16_ref[...] = (g * _sigmoid(g)).astype(BF16)

    @pl.when(j >= S_MA)
    def _():
        o16_ref[...] = _sigmoid(project()).astype(BF16)


def _pick_tile(t, candidates):
    return next(c for c in candidates if t % c == 0)


def _in_proj(h_all, norm_g, w_bf, q_g, k_g, lb_param, layer):
    t = h_all.shape[0]
    tm = _pick_tile(t, (1024, 512, 256, 128))
    depth = lb_param.shape[0]
    table = jnp.array([STEP_GROUPS, STEP_P32, STEP_P16], jnp.int32)
    grid_spec = pltpu.PrefetchScalarGridSpec(
        num_scalar_prefetch=1,
        grid=(t // tm, N_SPLIT),
        in_specs=[
            pl.BlockSpec((tm, D_MODEL), lambda i, j, tab: (i, 0)),
            pl.BlockSpec((1, D_MODEL), lambda i, j, tab: (0, 0)),
            pl.BlockSpec((D_MODEL, WIDTH), lambda i, j, tab: (0, tab[0, j])),
            pl.BlockSpec((1, HEAD_DIM), lambda i, j, tab: (0, 0)),
            pl.BlockSpec((1, HEAD_DIM), lambda i, j, tab: (0, 0)),
            pl.BlockSpec((depth, WIDTH), lambda i, j, tab: (0, 0)),
        ],
        out_specs=[
            pl.BlockSpec((tm, WIDTH), lambda i, j, tab: (i, tab[1, j])),
            pl.BlockSpec((tm, WIDTH), lambda i, j, tab: (i, tab[2, j])),
            pl.BlockSpec((tm * HEADS, HEAD_DIM), lambda i, j, tab: (i, 0)),
            pl.BlockSpec((tm * HEADS, HEAD_DIM), lambda i, j, tab: (i, 0)),
        ],
        scratch_shapes=[pltpu.VMEM((tm, D_MODEL), BF16)],
    )
    return pl.pallas_call(
        functools.partial(_in_proj_kernel, layer=layer),
        grid_spec=grid_spec,
        out_shape=[jax.ShapeDtypeStruct((t, N_P32 * WIDTH), F32),
                   jax.ShapeDtypeStruct((t, N_P16 * WIDTH), BF16),
                   jax.ShapeDtypeStruct((t * HEADS, HEAD_DIM), F32),
                   jax.ShapeDtypeStruct((t * HEADS, HEAD_DIM), F32)],
        compiler_params=_params("arbitrary", "arbitrary"),
        name="in_proj",
    )(table, h_all, norm_g.reshape(1, D_MODEL), w_bf, q_g.reshape(1, HEAD_DIM), k_g.reshape(1, HEAD_DIM),
      lb_param)


def _pool_kernel(*refs, nseq, seq, pos0, has_prefix):
    if has_prefix:
        u_ref, p_ref, w_ref, s_ref, o_ref = refs
    else:
        u_ref, w_ref, s_ref, o_ref = refs
    g = pl.program_id(1)
    u = u_ref[...].reshape(nseq, seq, POOL_GW)
    head = p_ref[...] if has_prefix else jnp.zeros((nseq, POOL_MAX, POOL_GW), F32)
    x = jnp.concatenate([head, u], axis=1)
    pos = pos0 + lax.broadcasted_iota(jnp.int32, (1, seq, 1), 1)

    for gi, win in enumerate(POOL_WINDOWS):
        @pl.when(g == gi)
        def _(win=win):
            s = x
            sh = 1
            while sh < win:
                s = s + pltpu.roll(s, sh, 1)
                sh *= 2
            cnt = jnp.minimum(pos + 1, win).astype(F32)
            diff = (s[:, POOL_MAX:] / cnt - u).reshape(nseq * seq, POOL_GW)
            y = jnp.dot(diff.astype(BF16), w_ref[0], preferred_element_type=F32)
            o_ref[...] = (y * s_ref[...]).astype(o_ref.dtype)


def _pool(proj32, prefix16, w_pool_bf, scale, *, nb, seq, nseq, row0, pos0):
    rows = nseq * seq
    rb0 = row0 // rows
    has_prefix = prefix16 is not None
    ua0 = P32_UA * len(POOL_WINDOWS)
    in_specs = [pl.BlockSpec((rows, POOL_GW), lambda b, g: (rb0 + b, ua0 + g))]
    args = [proj32]
    if has_prefix:
        in_specs.append(pl.BlockSpec((nseq, POOL_MAX, POOL_GW), lambda b, g: (b, 0, g)))
        args.append(prefix16)
    in_specs += [
        pl.BlockSpec((1, POOL_GW, POOL_GW), lambda b, g: (g, 0, 0)),
        pl.BlockSpec((1, POOL_GW), lambda b, g: (0, g)),
    ]
    args += [w_pool_bf, scale.reshape(1, WIDTH)]
    return pl.pallas_call(
        functools.partial(_pool_kernel, nseq=nseq, seq=seq, pos0=pos0, has_prefix=has_prefix),
        grid=(nb // nseq, len(POOL_WINDOWS)),
        in_specs=in_specs,
        out_specs=pl.BlockSpec((rows, POOL_GW), lambda b, g: (b, g)),
        out_shape=jax.ShapeDtypeStruct((nb * seq, WIDTH), BF16),
        compiler_params=_params("parallel", "parallel"),
        name="pool",
    )(*args)


def _tri_after(n):
    r = lax.broadcasted_iota(jnp.int32, (n, n), 0)
    c = lax.broadcasted_iota(jnp.int32, (n, n), 1)
    return jnp.where(r > c, 1.0, 0.0).astype(BF16)


def _causal_mask(tq, tk, shift):
    r = lax.broadcasted_iota(jnp.int32, (tq, tk), 0)
    c = lax.broadcasted_iota(jnp.int32, (tq, tk), 1)
    return c < r + shift


def _sb_heads(q_ref, key, val, tk, tri, carry_ref, keep_ref, logit_ref, o_ref, mask=None):
    tq = q_ref.shape[0]
    first = mask is not None
    for h in range(HEADS):
        rows = slice(h * tq, (h + 1) * tq)
        z = lax.dot_general(q_ref[:, _head(h)], key(h), (((1,), (1,)), ((), ())),
                            preferred_element_type=F32)
        soft = jnp.maximum(z, 0.0) + jnp.log2(1.0 + jnp.exp2(-jnp.abs(z)))
        if first:
            logit_ref[rows, :tk] = z - soft
            soft = jnp.where(mask, soft, 0.0)
            carry_ref[h] = jnp.sum(soft, axis=1, keepdims=True)
        else:
            carry = carry_ref[h]
            logit_ref[rows, :tk] = z - soft - carry
            carry_ref[h] = carry + jnp.sum(soft, axis=1, keepdims=True)
        keep_ref[rows, :tk] = soft.astype(BF16)
    later = jnp.dot(keep_ref[:, :tk], tri, preferred_element_type=F32)
    for h in range(HEADS):
        rows = slice(h * tq, (h + 1) * tq)
        wts = jnp.exp2(logit_ref[rows, :tk] - later[rows])
        if first:
            wts = jnp.where(mask, wts, 0.0)
        pv = jnp.dot(wts.astype(BF16), val(h), preferred_element_type=F32)
        if first:
            o_ref[:, _head(h)] = pv
        else:
            o_ref[:, _head(h)] += pv


def _attn_scratch(tq):
    return [pltpu.VMEM((tq, WIDTH), F32),
            pltpu.VMEM((TK_WIDE, TK_WIDE), BF16),
            pltpu.VMEM((HEADS, tq, 1), F32),
            pltpu.VMEM((HEADS * tq, TK_WIDE), BF16),
            pltpu.VMEM((HEADS * tq, TK_WIDE), F32)]


def _attn_prompt_kernel(q_ref, k_ref, v_ref, o_ref, acc_ref, triw_ref, carry_ref, keep_ref, logit_ref):
    i = pl.program_id(1)
    tq = q_ref.shape[0]

    @pl.when(i == 0)
    def _():
        triw_ref[...] = _tri_after(TK_WIDE)

    d0 = pl.multiple_of(i * tq, tq)
    odd = jnp.bitwise_and(i, 1)

    def sweep(k0, tk, tri, mask=None):
        _sb_heads(q_ref, lambda h: k_ref[pl.ds(k0, tk), _head(h)], lambda h: v_ref[pl.ds(k0, tk), _head(h)],
                  tk, tri, carry_ref, keep_ref, logit_ref, acc_ref, mask)

    @pl.when(odd == 0)
    def _():
        sweep(d0, tq, _tri_after(tq), _causal_mask(tq, tq, 0))

    @pl.when(odd == 1)
    def _():
        sweep(pl.multiple_of(d0 - tq, tq), TK_WIDE, triw_ref[...], _causal_mask(tq, TK_WIDE, tq))

    first_wide = d0 - odd * tq

    @pl.loop(0, lax.shift_right_logical(i, 1))
    def _(t):
        sweep(pl.multiple_of(first_wide - (t + 1) * TK_WIDE, tq), TK_WIDE, triw_ref[...])

    o_ref[...] = acc_ref[...].astype(o_ref.dtype)


def _attn_prompt(proj16, *, nb, seq):
    nq = seq // LANE
    return pl.pallas_call(
        _attn_prompt_kernel,
        grid=(nb, nq),
        in_specs=[
            pl.BlockSpec((LANE, WIDTH), lambda b, i: (b * nq + i, P16_QB)),
            pl.BlockSpec((seq, WIDTH), lambda b, i: (b, P16_KB)),
            pl.BlockSpec((seq, WIDTH), lambda b, i: (b, P16_VB)),
        ],
        out_specs=pl.BlockSpec((LANE, WIDTH), lambda b, i: (b * nq + i, 0)),
        out_shape=jax.ShapeDtypeStruct((nb * seq, WIDTH), BF16),
        scratch_shapes=_attn_scratch(LANE),
        compiler_params=_params("parallel", "arbitrary"),
        name="attn_prompt",
    )(proj16, proj16, proj16)


def _attn_sample_kernel(q_ref, kn_ref, vn_ref, kp_ref, vp_ref, o_ref, acc_ref, triw_ref, carry_ref, keep_ref,
                        logit_ref, *, past):
    tq = q_ref.shape[0]
    triw_ref[...] = _tri_after(TK_WIDE)
    n_past = past // TK_WIDE

    _sb_heads(q_ref, lambda h: kn_ref[:, _head(h)], lambda h: vn_ref[:, _head(h)],
              tq, _tri_after(tq), carry_ref, keep_ref, logit_ref, acc_ref, _causal_mask(tq, tq, 0))

    @pl.loop(0, n_past)
    def _(t):
        k0 = (n_past - 1 - t) * TK_WIDE

        def head_rows(ref, h):
            rows = pl.ds(pl.multiple_of(k0 * HEADS, HEADS) + h, TK_WIDE, stride=HEADS)
            return ref[0, 0, rows, :].astype(BF16)

        _sb_heads(q_ref, lambda h: head_rows(kp_ref, h), lambda h: head_rows(vp_ref, h),
                  TK_WIDE, triw_ref[...], carry_ref, keep_ref, logit_ref, acc_ref)

    o_ref[...] = acc_ref[...].astype(o_ref.dtype)


def _attn_sample(proj16, cache_k, cache_v, layer, *, nb, seq, row0):
    rows = cache_k.shape[2]
    rb0 = row0 // seq
    return pl.pallas_call(
        functools.partial(_attn_sample_kernel, past=rows // HEADS),
        grid=(nb,),
        in_specs=[
            pl.BlockSpec((seq, WIDTH), lambda b: (rb0 + b, P16_QB)),
            pl.BlockSpec((seq, WIDTH), lambda b: (rb0 + b, P16_KB)),
            pl.BlockSpec((seq, WIDTH), lambda b: (rb0 + b, P16_VB)),
            pl.BlockSpec((1, 1, rows, HEAD_DIM), lambda b: (layer, b, 0, 0)),
            pl.BlockSpec((1, 1, rows, HEAD_DIM), lambda b: (layer, b, 0, 0)),
        ],
        out_specs=pl.BlockSpec((seq, WIDTH), lambda b: (b, 0)),
        out_shape=jax.ShapeDtypeStruct((nb * seq, WIDTH), BF16),
        scratch_shapes=_attn_scratch(seq),
        compiler_params=_params("parallel"),
        name="attn_sample",
    )(proj16, proj16, proj16, cache_k, cache_v)


def _group_roll(x, d):
    n, lanes = x.shape
    return pltpu.roll(x.reshape(n // SUBLANES, SUBLANES, lanes), d, 1).reshape(n, lanes)


def _cumsum_rows(x, sub):
    n = x.shape[0]
    sh = 1
    while sh < SUBLANES:
        x = x + jnp.where(sub >= sh, _group_roll(x, sh), 0.0)
        sh *= 2
    parts, run = [], None
    for g in range(n // SUBLANES):
        blk = x[g * SUBLANES:(g + 1) * SUBLANES]
        if run is not None:
            blk = blk + run
        parts.append(blk)
        run = blk[SUBLANES - 1:SUBLANES]
    return jnp.concatenate(parts, axis=0)


def _half_end_rows(x, half):
    n, lanes = x.shape
    blk = 2 * half
    if blk <= SUBLANES:
        x3 = x.reshape(n // SUBLANES, SUBLANES, lanes)
        parts = [jnp.broadcast_to(x3[:, s + half - 1:s + half, :], (n // SUBLANES, blk, lanes))
                 for s in range(0, SUBLANES, blk)]
        return jnp.concatenate(parts, axis=1).reshape(n, lanes)
    return jnp.concatenate([jnp.broadcast_to(x[s + half - 1:s + half], (blk, lanes))
                            for s in range(0, n, blk)], axis=0)


def _pair_classes(n):
    r = lax.broadcasted_iota(jnp.int32, (n, n), 0)
    c = lax.broadcasted_iota(jnp.int32, (n, n), 1)

    def block_of(x, size):
        return jnp.bitwise_and(x, -size)

    same = block_of(r, HG_SUB) == block_of(c, HG_SUB)
    delta = jnp.where(jnp.logical_and(same, c <= r), r - c, -1)
    level = jnp.full((n, n), -1, jnp.int32)
    half, m = HG_SUB, 0
    while half < n:
        blk = 2 * half
        hit = jnp.logical_and(block_of(r, blk) == block_of(c, blk),
                              jnp.logical_and(jnp.bitwise_and(r, half) != 0, jnp.bitwise_and(c, half) == 0))
        level = jnp.where(hit, m, level)
        half, m = blk, m + 1
    return delta, level


def _hgrn_kernel(*refs, valid_len, has_state):
    refs = list(refs)
    f_ref, q_ref, k_ref, v_ref, ng_ref = refs[:5]
    s0_ref = refs[5] if has_state else None
    o_ref, sf_ref, st_ref, pair_ref, lvl_ref, att_ref, qdec_ref, kdec_ref, dec_ref = refs[-9:]
    c = pl.program_id(1)
    nc = pl.num_programs(1)
    n = HG_CHUNK

    row = lax.broadcasted_iota(jnp.int32, (n, 1), 0)
    valid = None if valid_len is None else (c * n + row) < valid_len
    sub = jnp.bitwise_and(row, SUBLANES - 1)
    delta, level = _pair_classes(n)
    is_delta = [delta == d for d in range(HG_SUB)]
    halves = []
    half = HG_SUB
    while half < n:
        halves.append(half)
        half *= 2
    is_level = [level == m for m in range(len(halves))]
    is_second = [jnp.bitwise_and(row, half) != 0 for half in halves]

    @pl.when(c == 0)
    def _():
        for h in range(HEADS):
            st_ref[h] = s0_ref[0, 0, h].T if has_state else jnp.zeros((HEAD_DIM, HEAD_DIM), F32)

    for h in range(HEADS):
        hs = _head(h)
        rows = slice(h * n, (h + 1) * n)
        log_f = f_ref[:, hs]
        kk = k_ref[:, hs].astype(F32)
        qs = q_ref[:, hs]
        if valid_len is not None:
            log_f = jnp.where(valid, log_f, 0.0)
            kk = jnp.where(valid, kk, 0.0)
        cum = _cumsum_rows(log_f, sub)

        att = jnp.where(is_delta[0], jnp.sum(qs * kk, axis=-1, keepdims=True), 0.0)
        for d in range(1, HG_SUB):
            w = jnp.exp2(jnp.minimum(cum - _group_roll(cum, d), 0.0))
            a_d = jnp.sum(qs * _group_roll(kk, d) * w, axis=-1, keepdims=True)
            att = jnp.where(is_delta[d], a_d, att)
        pair_ref[h] = att

        for m, half in enumerate(halves):
            x = jnp.where(is_second[m], qs, kk) * jnp.exp2(-jnp.abs(cum - _half_end_rows(cum, half)))
            lvl_ref[m, rows] = x.astype(BF16)

        last = cum[n - 1:n]
        qdec_ref[rows] = (qs * jnp.exp2(cum)).astype(BF16)
        kdec_ref[rows] = (kk * jnp.exp2(last - cum)).astype(BF16)
        dec_ref[h] = jnp.exp2(last)

    for h in range(HEADS):
        rows = slice(h * n, (h + 1) * n)
        att = pair_ref[h]
        for m in range(len(halves)):
            x = lvl_ref[m, rows]
            a_m = lax.dot_general(x, x, (((1,), (1,)), ((), ())), preferred_element_type=F32)
            att = jnp.where(is_level[m], a_m, att)
        att_ref[h] = att.astype(BF16)

    for h in range(HEADS):
        hs = _head(h)
        rows = slice(h * n, (h + 1) * n)
        v_bf = v_ref[:, hs]
        st = st_ref[h]
        o = jnp.dot(att_ref[h], v_bf, preferred_element_type=F32)
        o = o + lax.dot_general(qdec_ref[rows], st.astype(BF16), (((1,), (1,)), ((), ())),
                                preferred_element_type=F32)
        st_ref[h] = st * dec_ref[h] + lax.dot_general(v_bf, kdec_ref[rows], (((0,), (0,)), ((), ())),
                                                      preferred_element_type=F32)
        ms = jnp.mean(o * o, axis=-1, keepdims=True)
        o_ref[:, hs] = (o * lax.rsqrt(ms + EPS) * ng_ref[...]).astype(o_ref.dtype)

    @pl.when(c == nc - 1)
    def _():
        for h in range(HEADS):
            sf_ref[0, h] = st_ref[h].T


def _hgrn(proj32, proj16, norm_g, state0, layer, *, nb, seq, row0, valid_len):
    n = HG_CHUNK
    nc = seq // n
    rb0 = row0 // n
    has_state = state0 is not None
    n_levels = (n // HG_SUB).bit_length() - 1

    def col(cidx):
        return pl.BlockSpec((n, WIDTH), lambda b, c: (rb0 + b * nc + c, cidx))

    in_specs = [col(P32_FC), col(P32_QC), col(P16_KC), col(P16_IC),
                pl.BlockSpec((1, HEAD_DIM), lambda b, c: (0, 0))]
    args = [proj32, proj32, proj16, proj16, norm_g.reshape(1, HEAD_DIM)]
    if has_state:
        in_specs.append(pl.BlockSpec((1, 1, HEADS, HEAD_DIM, HEAD_DIM), lambda b, c: (layer, b, 0, 0, 0)))
        args.append(state0)
    return pl.pallas_call(
        functools.partial(_hgrn_kernel, valid_len=valid_len, has_state=has_state),
        grid=(nb, nc),
        in_specs=in_specs,
        out_specs=[
            pl.BlockSpec((n, WIDTH), lambda b, c: (b * nc + c, 0)),
            pl.BlockSpec((1, HEADS, HEAD_DIM, HEAD_DIM), lambda b, c: (b, 0, 0, 0)),
        ],
        out_shape=[
            jax.ShapeDtypeStruct((nb * seq, WIDTH), BF16),
            jax.ShapeDtypeStruct((nb, HEADS, HEAD_DIM, HEAD_DIM), F32),
        ],
        scratch_shapes=[
            pltpu.VMEM((HEADS, HEAD_DIM, HEAD_DIM), F32),
            pltpu.VMEM((HEADS, n, n), F32),
            pltpu.VMEM((n_levels, HEADS * n, HEAD_DIM), BF16),
            pltpu.VMEM((HEADS, n, n), BF16),
            pltpu.VMEM((HEADS * n, HEAD_DIM), BF16),
            pltpu.VMEM((HEADS * n, HEAD_DIM), BF16),
            pltpu.VMEM((HEADS, 1, HEAD_DIM), F32),
        ],
        compiler_params=_params("parallel", "arbitrary"),
        name="hgrn",
    )(*args)


def _merge_kernel(h_ref, yap_ref, ybp_ref, ycp_ref, yas_ref, ybs_ref, ycs_ref, ga_ref, gb_ref, gc_ref,
                  ma_ref, mb_ref, mc_ref, wb_ref, wo_ref, *o_refs, prompt_tiles):
    is_prompt = pl.program_id(0) < prompt_tiles
    merged = None
    for n, (yp_ref, ys_ref, g_ref, m_ref) in enumerate(((yap_ref, yas_ref, ga_ref, ma_ref),
                                                        (ybp_ref, ybs_ref, gb_ref, mb_ref),
                                                        (ycp_ref, ycs_ref, gc_ref, mc_ref))):
        y = jnp.where(is_prompt, yp_ref[...], ys_ref[...]).astype(F32)
        y = (y * g_ref[...].astype(F32)).astype(BF16)
        p = m_ref[...].astype(F32) * jnp.dot(y, wb_ref[n], preferred_element_type=F32)
        merged = p if merged is None else merged + p
    out = h_ref[...] + jnp.dot(merged.astype(BF16), wo_ref[...], preferred_element_type=F32)
    if len(o_refs) == 1:
        o_refs[0][...] = out
    else:
        @pl.when(is_prompt)
        def _():
            o_refs[0][...] = out

        @pl.when(jnp.logical_not(is_prompt))
        def _():
            o_refs[1][...] = out


def _merge(h_all, y_prompt, y_sample, proj16, wb_bf, wo_bf, split):
    t = h_all.shape[0]
    tp, ts = y_prompt[0].shape[0], y_sample[0].shape[0]
    tm = next(c for c in (256, 128) if tp % c == 0 and ts % c == 0)
    assert tp + ts == t
    n_p = tp // tm
    row = pl.BlockSpec((tm, WIDTH), lambda i: (i, 0))
    prow = pl.BlockSpec((tm, WIDTH), lambda i: (jnp.minimum(i, n_p - 1), 0))
    srow = pl.BlockSpec((tm, WIDTH), lambda i: (jnp.maximum(i - n_p, 0), 0))

    def col(cidx):
        return pl.BlockSpec((tm, WIDTH), lambda i: (i, cidx))

    if split:
        out_specs = [prow, srow]
        out_shape = [jax.ShapeDtypeStruct((tp, D_MODEL), F32), jax.ShapeDtypeStruct((ts, D_MODEL), F32)]
    else:
        out_specs = row
        out_shape = jax.ShapeDtypeStruct((t, D_MODEL), F32)
    return pl.pallas_call(
        functools.partial(_merge_kernel, prompt_tiles=n_p),
        grid=(t // tm,),
        in_specs=[row, prow, prow, prow, srow, srow, srow,
                  col(P16_GA), col(P16_GB), col(P16_GC), col(P16_MA), col(P16_MB), col(P16_MC),
                  pl.BlockSpec((3, WIDTH, D_MODEL), lambda i: (0, 0, 0)),
                  pl.BlockSpec((D_MODEL, D_MODEL), lambda i: (0, 0))],
        out_specs=out_specs,
        out_shape=out_shape,
        compiler_params=_params("arbitrary"),
        name="merge",
    )(h_all, *y_prompt, *y_sample, *([proj16] * 6), wb_bf, wo_bf)


def kernel(x_prompt, x_sample, cache_k, cache_v, state_pool, state_hgrn, meta_tokens, norm_g, w_in,
           q_norm_g, k_norm_g, w_pool, pool_scale, hgrn_lower_bounds, hgrn_norm_g, w_branch, w_out):
    bp, seq, _ = x_prompt.shape
    bs, dseq, _ = x_sample.shape
    depth = w_in.shape[0]
    past = cache_k.shape[2]
    lp_real = N_META + seq
    lp = -(-lp_real // LANE) * LANE
    tp = bp * lp
    ts = bs * dseq

    pad_rows = jnp.zeros((lp - lp_real, D_MODEL), F32)
    pieces = []
    for b in range(bp):
        pieces += [meta_tokens.astype(F32), x_prompt[b], pad_rows]
    h_all = jnp.concatenate(pieces + [x_sample.reshape(ts, D_MODEL)], axis=0)

    cache_k2 = cache_k.reshape(depth, bs, past * HEADS, HEAD_DIM)
    cache_v2 = cache_v.reshape(depth, bs, past * HEADS, HEAD_DIM)
    prefix16 = jnp.pad(state_pool, ((0, 0), (0, 0), (1, 0), (0, 0)))

    kp, vp, pp, sp, ksm, vsm, psm, ssm = [], [], [], [], [], [], [], []
    for l in range(depth):
        proj32, proj16, keys, vals = _in_proj(h_all, norm_g[l], w_in[l].astype(BF16), q_norm_g[l], k_norm_g[l],
                                              hgrn_lower_bounds, l)
        w_pool_bf = w_pool[l].astype(BF16)

        ya_p = _pool(proj32, None, w_pool_bf, pool_scale[l], nb=bp, seq=lp, nseq=1, row0=0, pos0=0)
        yb_p = _attn_prompt(proj16, nb=bp, seq=lp)
        yc_p, st_p = _hgrn(proj32, proj16, hgrn_norm_g[l], None, l,
                           nb=bp, seq=lp, row0=0, valid_len=lp_real)
        ya_s = _pool(proj32, prefix16[l], w_pool_bf, pool_scale[l], nb=bs, seq=dseq,
                     nseq=math.gcd(bs, tp // dseq), row0=tp, pos0=past)
        yb_s = _attn_sample(proj16, cache_k2, cache_v2, l, nb=bs, seq=dseq, row0=tp)
        yc_s, st_s = _hgrn(proj32, proj16, hgrn_norm_g[l], state_hgrn, l,
                           nb=bs, seq=dseq, row0=tp, valid_len=None)

        h_all = _merge(h_all, (ya_p, yb_p, yc_p), (ya_s, yb_s, yc_s), proj16,
                       w_branch[l].astype(BF16), w_out[l].astype(BF16), split=(l == depth - 1))

        def pool_rows(r0, r1):
            return proj32[r0:r1, P32_UA * WIDTH:(P32_UA + 1) * WIDTH]

        def token_rows(a, r0, r1):
            return a[r0 * HEADS:r1 * HEADS].reshape(r1 - r0, HEADS, HEAD_DIM)

        kp += [token_rows(keys, b * lp, b * lp + lp_real) for b in range(bp)]
        vp += [token_rows(vals, b * lp, b * lp + lp_real) for b in range(bp)]
        pp.append(jnp.stack([pool_rows(b * lp + lp_real - (POOL_MAX - 1), b * lp + lp_real) for b in range(bp)]))
        sp.append(st_p)
        ksm.append(keys[tp * HEADS:].reshape(bs, dseq, HEADS, HEAD_DIM))
        vsm.append(vals[tp * HEADS:].reshape(bs, dseq, HEADS, HEAD_DIM))
        full_s = jnp.concatenate([state_pool[l], pool_rows(tp, tp + ts).reshape(bs, dseq, WIDTH)], axis=1)
        psm.append(full_s[:, -(POOL_MAX - 1):])
        ssm.append(st_s)

    h_prompt, h_sample = h_all
    y_prompt = h_prompt.reshape(bp, lp, D_MODEL)[:, N_META:lp_real]
    y_sample = h_sample.reshape(bs, dseq, D_MODEL)
    k_prompt = jnp.stack(kp).reshape(depth, bp, lp_real, HEADS, HEAD_DIM)
    v_prompt = jnp.stack(vp).reshape(depth, bp, lp_real, HEADS, HEAD_DIM)
    return (y_prompt, y_sample, k_prompt, v_prompt, jnp.stack(pp), jnp.stack(sp),
            jnp.stack(ksm), jnp.stack(vsm), jnp.stack(psm), jnp.stack(ssm))
```
